```python
import math
import jax, jax.numpy as jnp
from jax import lax
import numpy as np

D_MODEL = 1024
BATCH = 32
SEQ = 256
DEPTH = 4
DEC_BATCH = 4
DEC_SEQ = 2048
PAST_LEN = 256

GRID_W = 64
HEAD_DIM = 64
DIFF_HEADS = 4
DIFF_V_DIM = 2 * HEAD_DIM
MLA_HEADS = 8
MLA_Q_RANK = 256
MLA_KV_RANK = 128
MLA_NOPE_DIM = 64
MLA_ROPE_DIM = 32
MLA_QK_DIM = MLA_NOPE_DIM + MLA_ROPE_DIM
MLA_V_DIM = 64
GQA_HEADS = 8
GQA_KV_HEADS = 2
GQA_REP = GQA_HEADS // GQA_KV_HEADS
D_FF = 2816
CONV_WIDTH = 3
ROPE_BASE = 10000.0
NORM_EPS = 1e-6
Q_BLOCK = 128
N_BRANCH = 3

A_WIDTH = DIFF_HEADS * DIFF_V_DIM
B_WIDTH = MLA_HEADS * MLA_V_DIM
C_WIDTH = GQA_HEADS * HEAD_DIM

IN_SIZES = (2 * DIFF_HEADS * HEAD_DIM,
            2 * DIFF_HEADS * HEAD_DIM,
            DIFF_HEADS * DIFF_V_DIM,
            MLA_Q_RANK,
            MLA_KV_RANK + MLA_ROPE_DIM,
            GQA_HEADS * HEAD_DIM,
            GQA_KV_HEADS * HEAD_DIM,
            GQA_KV_HEADS * HEAD_DIM,
            N_BRANCH * D_MODEL)
IN_COLS = sum(IN_SIZES)
IN_SPLITS = tuple(int(s) for s in np.cumsum(IN_SIZES)[:-1])

kernel_name = 'hybrid_diffusion_prefix_step'


def rms_norm(x, g):
    xf = x.astype(jnp.float32)
    y = xf * lax.rsqrt(jnp.mean(xf * xf, axis=-1, keepdims=True) + NORM_EPS)
    return (y * g.astype(jnp.float32)).astype(x.dtype)


def axial_rope_tables(n_tokens, dim):
    rows = n_tokens // GRID_W
    row = jnp.repeat(jnp.arange(rows, dtype=jnp.float32), GRID_W)
    col = jnp.tile(jnp.arange(GRID_W, dtype=jnp.float32), rows)
    quarter = dim // 4
    inv = 1.0 / (ROPE_BASE ** (jnp.arange(quarter, dtype=jnp.float32) / quarter))
    ang = jnp.concatenate([row[:, None] * inv, col[:, None] * inv], axis=-1)
    return jnp.cos(ang), jnp.sin(ang)


def apply_rope(x, cos, sin):
    half = x.shape[-1] // 2
    shape = (x.shape[1],) + (1,) * (x.ndim - 3) + (half,)
    cos = cos.reshape(shape).astype(x.dtype)
    sin = sin.reshape(shape).astype(x.dtype)
    x1, x2 = x[..., :half], x[..., half:]
    return jnp.concatenate([x1 * cos - x2 * sin, x1 * sin + x2 * cos], axis=-1)


def over_query_blocks(fn, q):
    b, t = q.shape[:2]
    nb = t // Q_BLOCK
    blocks = jnp.moveaxis(q.reshape((b, nb, Q_BLOCK) + q.shape[2:]), 1, 0)
    out = jnp.moveaxis(lax.map(fn, blocks), 0, 1)
    return out.reshape((b, t) + out.shape[3:])


def gqa_block(q, k, v):
    scale = q.shape[-1] ** -0.5
    s = jnp.einsum('bqgrd,bsgd->bgrqs', q, k).astype(jnp.float32) * scale
    p = jax.nn.softmax(s, axis=-1).astype(v.dtype)
    return jnp.einsum('bgrqs,bsge->bqgre', p, v)


def diff_block(q, k, v, lam):
    scale = q.shape[-1] ** -0.5
    s = jnp.einsum('bqmhd,bsmhd->bmhqs', q, k).astype(jnp.float32) * scale
    p = jax.nn.softmax(s, axis=-1)
    w = (p[:, 0] - lam * p[:, 1]).astype(v.dtype)
    return jnp.einsum('bhqs,bshe->bqhe', w, v)


def mla_expand(ckv, kpe, p):
    b, s = ckv.shape[:2]
    kv = (rms_norm(ckv, p['mla_kv_norm']) @ p['w_mla_kv_up']).reshape(b, s, MLA_HEADS, MLA_NOPE_DIM + MLA_V_DIM)
    k_nope, v = kv[..., :MLA_NOPE_DIM], kv[..., MLA_NOPE_DIM:]
    k_pe = jnp.broadcast_to(kpe[:, :, None, :], (b, s, MLA_HEADS, MLA_ROPE_DIM))
    k = rms_norm(jnp.concatenate([k_nope, k_pe], axis=-1), p['mla_qk_norm'][1])
    return k, v


def rope_tail(x, cos, sin):
    return jnp.concatenate([x[..., :MLA_NOPE_DIM], apply_rope(x[..., MLA_NOPE_DIM:], cos, sin)], axis=-1)


def token_mixers(h, p, lam_init, rope, ctx):
    b, t, _ = h.shape
    aq, ak, av, bq_d, bkv_d, cq, ck, cv, gates = jnp.split(h @ p['w_in'], IN_SPLITS, axis=-1)
    aq = rms_norm(aq.reshape(b, t, 2, DIFF_HEADS, HEAD_DIM), p['diff_qk_norm'][0])
    ak = rms_norm(ak.reshape(b, t, 2, DIFF_HEADS, HEAD_DIM), p['diff_qk_norm'][1])
    av = av.reshape(b, t, DIFF_HEADS, DIFF_V_DIM)
    bq = (rms_norm(bq_d, p['mla_q_norm']) @ p['w_mla_q_up']).reshape(b, t, MLA_HEADS, MLA_QK_DIM)
    bq = rms_norm(bq, p['mla_qk_norm'][0])
    ckv, kpe = bkv_d[..., :MLA_KV_RANK], bkv_d[..., MLA_KV_RANK:]
    bk, bv = mla_expand(ckv, kpe, p)
    cq = rms_norm(cq.reshape(b, t, GQA_HEADS, HEAD_DIM), p['gqa_qk_norm'][0])
    ck = rms_norm(ck.reshape(b, t, GQA_KV_HEADS, HEAD_DIM), p['gqa_qk_norm'][1])
    cv = cv.reshape(b, t, GQA_KV_HEADS, HEAD_DIM)
    own_ctx = (ak, av, ckv, kpe, ck, cv)

    if rope is not None:
        (cos_h, sin_h), (cos_r, sin_r) = rope
        aq, ak = apply_rope(aq, cos_h, sin_h), apply_rope(ak, cos_h, sin_h)
        bq, bk = rope_tail(bq, cos_r, sin_r), rope_tail(bk, cos_r, sin_r)
        cq, ck = apply_rope(cq, cos_h, sin_h), apply_rope(ck, cos_h, sin_h)
        c_ak, c_av, c_ckv, c_kpe, c_ck, c_cv = ctx
        c_bk, c_bv = mla_expand(c_ckv, c_kpe, p)
        ak = jnp.concatenate([c_ak, ak], axis=1)
        av = jnp.concatenate([c_av, av], axis=1)
        bk = jnp.concatenate([c_bk, bk], axis=1)
        bv = jnp.concatenate([c_bv, bv], axis=1)
        ck = jnp.concatenate([c_ck, ck], axis=1)
        cv = jnp.concatenate([c_cv, cv], axis=1)

    lam_vecs = p['diff_lambda'].astype(jnp.float32)
    lam = (jnp.exp(jnp.sum(lam_vecs[0] * lam_vecs[1])) - jnp.exp(jnp.sum(lam_vecs[2] * lam_vecs[3])) + lam_init)
    oa = over_query_blocks(lambda qb: diff_block(qb, ak, av, lam), aq)
    oa = (rms_norm(oa, p['diff_out_norm']) * (1.0 - lam_init)).reshape(b, t, A_WIDTH)
    ob = over_query_blocks(lambda qb: gqa_block(qb, bk, bv), bq[:, :, :, None, :]).reshape(b, t, B_WIDTH)
    oc = over_query_blocks(lambda qb: gqa_block(qb, ck, cv),
                           cq.reshape(b, t, GQA_KV_HEADS, GQA_REP, HEAD_DIM)).reshape(b, t, C_WIDTH)

    g = jax.nn.sigmoid(gates.reshape(b, t, N_BRANCH, D_MODEL))
    merged = (g[:, :, 0] * (oa @ p['w_branch_a']) + g[:, :, 1] * (ob @ p['w_branch_b'])
              + g[:, :, 2] * (oc @ p['w_branch_c']))
    return merged @ p['w_o'], own_ctx


def conv_ffn(h, p):
    u = h @ p['w_up']
    t = u.shape[1]
    pad = CONV_WIDTH // 2
    up = jnp.pad(u, ((0, 0), (pad, pad), (0, 0)))
    w = p['conv_w']
    u = up[:, 0:t] * w[0] + up[:, 1:t + 1] * w[1] + up[:, 2:t + 2] * w[2] + p['conv_b']
    a, gt = u[..., :D_FF], u[..., D_FF:]
    return (a * jax.nn.silu(gt)) @ p['w_down']


def trunk_layer(x, mod, p, lam_init, rope, ctx):
    shift_m, scale_m, gate_m, shift_f, scale_f, gate_f = jnp.split(mod, 6, axis=-1)
    h = rms_norm(x, p['norm_mix']) * (1.0 + scale_m) + shift_m
    mix, own_ctx = token_mixers(h, p, lam_init, rope, ctx)
    x = x + gate_m * mix
    h = rms_norm(x, p['norm_ffn']) * (1.0 + scale_f) + shift_f
    x = x + gate_f * conv_ffn(h, p)
    return x, own_ctx


def setup_inputs(seed: int = 0) -> dict:
    key = jax.random.key(seed)
    ks = iter(jax.random.split(key, 40))

    def nrm(shape, scale=1.0):
        return jax.random.normal(next(ks), shape, jnp.float32) * scale

    def gain(shape):
        return 1.0 + nrm(shape, 0.05)

    L, D = DEPTH, D_MODEL
    return {
        'x_prompt': nrm((BATCH, SEQ, D)),
        'x_sample': nrm((DEC_BATCH, DEC_SEQ, D)),
        'c': nrm((DEC_BATCH, D)),
        'cache_diff_k': nrm((DEC_BATCH, L, PAST_LEN, 2, DIFF_HEADS, HEAD_DIM)),
        'cache_diff_v': nrm((DEC_BATCH, L, PAST_LEN, DIFF_HEADS, DIFF_V_DIM)),
        'cache_mla_ckv': nrm((DEC_BATCH, L, PAST_LEN, MLA_KV_RANK)),
        'cache_mla_kpe': nrm((DEC_BATCH, L, PAST_LEN, MLA_ROPE_DIM)),
        'cache_gqa_k': nrm((DEC_BATCH, L, PAST_LEN, GQA_KV_HEADS, HEAD_DIM)),
        'cache_gqa_v': nrm((DEC_BATCH, L, PAST_LEN, GQA_KV_HEADS, HEAD_DIM)),
        'c_ctx': nrm((D,)),
        'w_ada': nrm((L, D, 6 * D), 0.5 * D ** -0.5),
        'b_ada': nrm((L, 6 * D), 0.02),
        'norm_mix': gain((L, D)),
        'w_in': nrm((L, D, IN_COLS), D ** -0.5),
        'diff_qk_norm': gain((L, 2, HEAD_DIM)),
        'diff_lambda': nrm((L, 4, HEAD_DIM), 0.1),
        'diff_out_norm': gain((L, DIFF_V_DIM)),
        'mla_q_norm': gain((L, MLA_Q_RANK)),
        'w_mla_q_up': nrm((L, MLA_Q_RANK, MLA_HEADS * MLA_QK_DIM), MLA_Q_RANK ** -0.5),
        'mla_kv_norm': gain((L, MLA_KV_RANK)),
        'w_mla_kv_up': nrm((L, MLA_KV_RANK, MLA_HEADS * (MLA_NOPE_DIM + MLA_V_DIM)), MLA_KV_RANK ** -0.5),
        'mla_qk_norm': gain((L, 2, MLA_QK_DIM)),
        'gqa_qk_norm': gain((L, 2, HEAD_DIM)),
        'w_branch_a': nrm((L, A_WIDTH, D), A_WIDTH ** -0.5),
        'w_branch_b': nrm((L, B_WIDTH, D), B_WIDTH ** -0.5),
        'w_branch_c': nrm((L, C_WIDTH, D), C_WIDTH ** -0.5),
        'w_o': nrm((L, D, D), D ** -0.5),
        'norm_ffn': gain((L, D)),
        'w_up': nrm((L, D, 2 * D_FF), D ** -0.5),
        'conv_w': nrm((L, CONV_WIDTH, 2 * D_FF), CONV_WIDTH ** -0.5),
        'conv_b': nrm((L, 2 * D_FF), 0.02),
        'w_down': nrm((L, D_FF, D), D_FF ** -0.5),
    }


def reference(x_prompt, x_sample, c, cache_diff_k, cache_diff_v, cache_mla_ckv, cache_mla_kpe,
              cache_gqa_k, cache_gqa_v, c_ctx, w_ada, b_ada, norm_mix, w_in, diff_qk_norm,
              diff_lambda, diff_out_norm, mla_q_norm, w_mla_q_up, mla_kv_norm, w_mla_kv_up,
              mla_qk_norm, gqa_qk_norm, w_branch_a, w_branch_b, w_branch_c, w_o, norm_ffn,
              w_up, conv_w, conv_b, w_down):
    t_lat = x_sample.shape[1]
    rope = (axial_rope_tables(t_lat, HEAD_DIM), axial_rope_tables(t_lat, MLA_ROPE_DIM))
    y_p, y_s = x_prompt, x_sample
    states = []
    for l in range(DEPTH):
        p = {
            'norm_mix': norm_mix[l], 'w_in': w_in[l], 'diff_qk_norm': diff_qk_norm[l],
            'diff_lambda': diff_lambda[l], 'diff_out_norm': diff_out_norm[l],
            'mla_q_norm': mla_q_norm[l], 'w_mla_q_up': w_mla_q_up[l],
            'mla_kv_norm': mla_kv_norm[l], 'w_mla_kv_up': w_mla_kv_up[l],
            'mla_qk_norm': mla_qk_norm[l], 'gqa_qk_norm': gqa_qk_norm[l],
            'w_branch_a': w_branch_a[l], 'w_branch_b': w_branch_b[l], 'w_branch_c': w_branch_c[l],
            'w_o': w_o[l], 'norm_ffn': norm_ffn[l], 'w_up': w_up[l], 'conv_w': conv_w[l],
            'conv_b': conv_b[l], 'w_down': w_down[l],
        }
        lam_init = 0.8 - 0.6 * math.exp(-0.3 * l)
        mod_ctx = (jax.nn.silu(c_ctx) @ w_ada[l] + b_ada[l])[None, None, :]
        mod_lat = (jax.nn.silu(c) @ w_ada[l] + b_ada[l])[:, None, :]
        y_p, own_ctx = trunk_layer(y_p, mod_ctx, p, lam_init, None, None)
        states.append(own_ctx)
        ctx = (cache_diff_k[:, l], cache_diff_v[:, l], cache_mla_ckv[:, l], cache_mla_kpe[:, l],
               cache_gqa_k[:, l], cache_gqa_v[:, l])
        y_s, _ = trunk_layer(y_s, mod_lat, p, lam_init, rope, ctx)
    new_diff_k = jnp.stack([s[0] for s in states], axis=1)
    new_diff_v = jnp.stack([s[1] for s in states], axis=1)
    new_mla_ckv = jnp.stack([s[2] for s in states], axis=1)
    new_mla_kpe = jnp.stack([s[3] for s in states], axis=1)
    new_gqa_k = jnp.stack([s[4] for s in states], axis=1)
    new_gqa_v = jnp.stack([s[5] for s in states], axis=1)
    return (y_p, y_s, new_diff_k, new_diff_v, new_mla_ckv, new_mla_kpe, new_gqa_k, new_gqa_v)
```

```python
import functools
import math

import jax
import jax.numpy as jnp
import numpy as np
from jax import lax
from jax.experimental import pallas as pl
from jax.experimental.pallas import tpu as pltpu

D_MODEL = 1024
BATCH = 32
SEQ = 256
DEPTH = 4
DEC_BATCH = 4
DEC_SEQ = 2048
PAST_LEN = 256
GRID_W = 64
HEAD_DIM = 64
DIFF_HEADS = 4
DIFF_V_DIM = 2 * HEAD_DIM
MLA_HEADS = 8
MLA_Q_RANK = 256
MLA_KV_RANK = 128
MLA_NOPE_DIM = 64
MLA_ROPE_DIM = 32
MLA_QK_DIM = MLA_NOPE_DIM + MLA_ROPE_DIM
MLA_V_DIM = 64
GQA_HEADS = 8
GQA_KV_HEADS = 2
D_FF = 2816
ROPE_BASE = 10000.0
NORM_EPS = 1e-6

LANES = 128
VMEM_LIMIT_BYTES = 56 * 1024 * 1024

F32 = jnp.float32
BF16 = jnp.bfloat16

C_AQ, C_AK, C_AV = 0, 512, 1024
C_BQD, C_CKV, C_KPE = 1536, 1792, 1920
C_CQ, C_CK, C_CV = 2048, 2560, 2816
W1_COLS = 3072

G_BQ, G_BK, G_AQ, G_AK, G_AV, G_BV, G_CQ, G_CK, G_CV = 0, 8, 16, 20, 24, 28, 32, 36, 38
N_GROUPS = 40

V_NORM_MIX, V_NORM_FFN, V_AQ, V_AK, V_BQD, V_BQ, V_CKV, V_BK, V_CQ, V_CK, V_AOUT = range(11)
N_VEC_ROWS = 16

M_SHIFT_M, M_SCALE_M, M_GATE_M, M_SHIFT_F, M_SCALE_F, M_GATE_F = range(6)
N_MOD_ROWS = 8

TM_IN = 256
TQ = 256
TM_MERGE = 512
TM_FFN = 2048
TF = 256
TN_ADA = 1536


def _cparams(sem):
    return pltpu.CompilerParams(dimension_semantics=sem, vmem_limit_bytes=VMEM_LIMIT_BYTES)


def _rms(x, gain):
    ms = jnp.mean(x * x, axis=-1, keepdims=True)
    return x * lax.rsqrt(ms + NORM_EPS) * gain


def _lane_lo(rows):
    return lax.broadcasted_iota(jnp.int32, (rows, LANES), 1) < HEAD_DIM


def _norm_heads64(xg, gain_g, lo):
    sq = xg * xg
    s_all = jnp.sum(sq, axis=-1, keepdims=True)
    s_lo = jnp.sum(jnp.where(lo, sq, 0.0), axis=-1, keepdims=True)
    ssq = jnp.where(lo, s_lo, s_all - s_lo)
    return xg * lax.rsqrt(ssq * (1.0 / HEAD_DIM) + NORM_EPS) * gain_g


def _norm_head96(xg, gain_g):
    ssq = jnp.sum(xg * xg, axis=-1, keepdims=True)
    return xg * lax.rsqrt(ssq * (1.0 / MLA_QK_DIM) + NORM_EPS) * gain_g


def _rope(xg, c, s_up, s_dn, half):
    return xg * c + pltpu.roll(xg, LANES - half, 1) * s_up + pltpu.roll(xg, half, 1) * s_dn


def _sigmoid(z):
    return 1.0 / (1.0 + jnp.exp(-z))


def _half_heads(qg, lo):
    qf = qg.astype(F32)
    return jnp.where(lo, qf, 0.0).astype(BF16), jnp.where(lo, 0.0, qf).astype(BF16)


def _mods_kernel(c_ref, w_ref, b_ref, o_ref):
    c = c_ref[...]
    a = c * _sigmoid(c)
    w = w_ref[...]
    a_hi = a.astype(BF16)
    a_lo = (a - a_hi.astype(F32)).astype(BF16)
    w_hi = w.astype(BF16)
    w_lo = (w - w_hi.astype(F32)).astype(BF16)
    acc = jnp.dot(a_hi, w_hi, preferred_element_type=F32)
    acc = acc + jnp.dot(a_lo, w_hi, preferred_element_type=F32)
    acc = acc + jnp.dot(a_hi, w_lo, preferred_element_type=F32)
    o_ref[...] = acc + b_ref[...]


def _mods(cvec, w_ada, b_ada):
    n = 6 * D_MODEL
    return pl.pallas_call(
        _mods_kernel,
        out_shape=jax.ShapeDtypeStruct((DEPTH, N_MOD_ROWS, n), F32),
        grid=(DEPTH, n // TN_ADA),
        in_specs=[
            pl.BlockSpec((N_MOD_ROWS, D_MODEL), lambda l, j: (0, 0)),
            pl.BlockSpec((None, D_MODEL, TN_ADA), lambda l, j: (l, 0, j)),
            pl.BlockSpec((None, 1, TN_ADA), lambda l, j: (l, 0, j)),
        ],
        out_specs=pl.BlockSpec((None, N_MOD_ROWS, TN_ADA), lambda l, j: (l, 0, j)),
        compiler_params=_cparams(("arbitrary", "arbitrary")),
        name="ada_mods",
    )(cvec, w_ada, b_ada.reshape(DEPTH, 1, n))


def _mla_expand(ckv, kpe_grp, wkv_ref, vec_ref):
    ckvn = _rms(ckv, vec_ref[V_CKV:V_CKV + 1, 0:MLA_KV_RANK]).astype(BF16)
    kv = jnp.dot(ckvn, wkv_ref[...], preferred_element_type=F32)
    keys = []
    for h in range(MLA_HEADS):
        kpre = kv[:, h * LANES:(h + 1) * LANES] + kpe_grp
        keys.append(_norm_head96(kpre, vec_ref[V_BK:V_BK + 1, h * LANES:(h + 1) * LANES]))
    return keys, kv[:, MLA_HEADS * LANES:]


def _ctx_mla_kernel(ckv_ref, kpe_ref, wkv_ref, vec_ref, k_ref, v_ref):
    keys, vals = _mla_expand(ckv_ref[...], kpe_ref[...], wkv_ref, vec_ref)
    for h in range(MLA_HEADS):
        k_ref[h] = keys[h].astype(BF16)
    for p in range(MLA_HEADS // 2):
        v_ref[p] = vals[:, p * LANES:(p + 1) * LANES].astype(BF16)


def _ctx_mla(cache_ckv, cache_kpe_grp, wkv, vecs):
    return pl.pallas_call(
        _ctx_mla_kernel,
        out_shape=(
            jax.ShapeDtypeStruct((DEPTH, DEC_BATCH, MLA_HEADS, PAST_LEN, LANES), BF16),
            jax.ShapeDtypeStruct((DEPTH, DEC_BATCH, MLA_HEADS // 2, PAST_LEN, LANES), BF16),
        ),
        grid=(DEPTH, DEC_BATCH),
        in_specs=[
            pl.BlockSpec((None, None, PAST_LEN, MLA_KV_RANK), lambda l, b: (b, l, 0, 0)),
            pl.BlockSpec((None, None, PAST_LEN, LANES), lambda l, b: (b, l, 0, 0)),
            pl.BlockSpec((None, MLA_KV_RANK, 12 * LANES), lambda l, b: (l, 0, 0)),
            pl.BlockSpec((None, N_VEC_ROWS, D_MODEL), lambda l, b: (l, 0, 0)),
        ],
        out_specs=(
            pl.BlockSpec((None, None, MLA_HEADS, PAST_LEN, LANES), lambda l, b: (l, b, 0, 0, 0)),
            pl.BlockSpec((None, None, MLA_HEADS // 2, PAST_LEN, LANES), lambda l, b: (l, b, 0, 0, 0)),
        ),
        compiler_params=_cparams(("arbitrary", "arbitrary")),
        name="ctx_mla_expand",
    )(cache_ckv, cache_kpe_grp, wkv, vecs)


def _inproj_kernel(latent, x_ref, mod_ref, vec_ref, w1_ref, wq_ref, wkv_ref, *rest):
    if latent:
        rope_ref, qkv_ref, p_scr = rest
    else:
        qkv_ref, st_ak, st_av, st_ckv, st_kpe, st_ck, st_cv, p_scr = rest
    rows = x_ref.shape[0]
    lo = _lane_lo(rows)

    x = x_ref[...]
    h = _rms(x, vec_ref[V_NORM_MIX:V_NORM_MIX + 1, :])
    h = h * (1.0 + mod_ref[M_SCALE_M:M_SCALE_M + 1, :]) + mod_ref[M_SHIFT_M:M_SHIFT_M + 1, :]
    p_scr[...] = jnp.dot(h.astype(BF16), w1_ref[...], preferred_element_type=F32)

    def grp(col, g):
        return p_scr[:, col + g * LANES:col + (g + 1) * LANES]

    def vec(row, g):
        return vec_ref[row:row + 1, g * LANES:(g + 1) * LANES]

    def rope64(y):
        if not latent:
            return y
        return _rope(y, rope_ref[0], rope_ref[1], rope_ref[2], HEAD_DIM // 2)

    def rope_tail(y):
        if not latent:
            return y
        return _rope(y, rope_ref[3], rope_ref[4], rope_ref[5], MLA_ROPE_DIM // 2)

    for g in range(4):
        qkv_ref[G_AQ + g] = rope64(_norm_heads64(grp(C_AQ, g), vec(V_AQ, g), lo)).astype(BF16)
        akn = _norm_heads64(grp(C_AK, g), vec(V_AK, g), lo)
        qkv_ref[G_AK + g] = rope64(akn).astype(BF16)
        av = grp(C_AV, g)
        qkv_ref[G_AV + g] = av.astype(BF16)
        if not latent:
            st_ak[:, g * LANES:(g + 1) * LANES] = akn
            st_av[:, g * LANES:(g + 1) * LANES] = av

    bqn = _rms(p_scr[:, C_BQD:C_BQD + MLA_Q_RANK], vec_ref[V_BQD:V_BQD + 1, 0:MLA_Q_RANK]).astype(BF16)
    bq = jnp.dot(bqn, wq_ref[...], preferred_element_type=F32)
    for hd in range(MLA_HEADS):
        qn = _norm_head96(bq[:, hd * LANES:(hd + 1) * LANES], vec(V_BQ, hd))
        qkv_ref[G_BQ + hd] = rope_tail(qn).astype(BF16)
    ckv = p_scr[:, C_CKV:C_CKV + MLA_KV_RANK]
    kpe_grp = p_scr[:, C_KPE:C_KPE + LANES]
    keys, vals = _mla_expand(ckv, kpe_grp, wkv_ref, vec_ref)
    for hd in range(MLA_HEADS):
        qkv_ref[G_BK + hd] = rope_tail(keys[hd]).astype(BF16)
    for p in range(MLA_HEADS // 2):
        qkv_ref[G_BV + p] = vals[:, p * LANES:(p + 1) * LANES].astype(BF16)
    if not latent:
        st_ckv[...] = ckv
        st_kpe[...] = kpe_grp

    for g in range(4):
        qkv_ref[G_CQ + g] = rope64(_norm_heads64(grp(C_CQ, g), vec(V_CQ, g), lo)).astype(BF16)
    ckn = [_norm_heads64(grp(C_CK, g), vec(V_CK, g), lo) for g in range(2)]
    cvs = [grp(C_CV, g) for g in range(2)]
    for g in range(2):
        qkv_ref[G_CK + g] = rope64(ckn[g]).astype(BF16)
        qkv_ref[G_CV + g] = cvs[g].astype(BF16)
    if not latent:
        st_ck[...] = jnp.where(lo, ckn[0], ckn[1])
        st_cv[...] = jnp.where(lo, cvs[0], cvs[1])


def _inproj(x, mod_l, vec_l, w1, wq, wkv, rope_tab, latent):
    t = x.shape[0]
    n_tiles = t // TM_IN
    tiles_per_seq = DEC_SEQ // TM_IN
    if latent:
        mod_idx = lambda i: (1 + i // tiles_per_seq, 0, 0)
    else:
        mod_idx = lambda i: (0, 0, 0)
    const2 = lambda i: (0, 0)
    in_specs = [
        pl.BlockSpec((TM_IN, D_MODEL), lambda i: (i, 0)),
        pl.BlockSpec((None, 6, D_MODEL), mod_idx),
        pl.BlockSpec((N_VEC_ROWS, D_MODEL), const2),
        pl.BlockSpec((D_MODEL, W1_COLS), const2),
        pl.BlockSpec((MLA_Q_RANK, MLA_HEADS * LANES), const2),
        pl.BlockSpec((MLA_KV_RANK, 12 * LANES), const2),
    ]
    args = [x, mod_l, vec_l, w1, wq, wkv]
    qkv_shape = jax.ShapeDtypeStruct((N_GROUPS, t, LANES), BF16)
    qkv_spec = pl.BlockSpec((N_GROUPS, TM_IN, LANES), lambda i: (0, i, 0))
    if latent:
        in_specs.append(pl.BlockSpec((6, TM_IN, LANES), lambda i: (0, i % tiles_per_seq, 0)))
        args.append(rope_tab)
        out_shape = qkv_shape
        out_specs = qkv_spec
    else:
        widths = (512, 512, LANES, LANES, LANES, LANES)
        out_shape = (qkv_shape,) + tuple(jax.ShapeDtypeStruct((t, w), F32) for w in widths)
        out_specs = (qkv_spec,) + tuple(pl.BlockSpec((TM_IN, w), lambda i: (i, 0)) for w in widths)
    return pl.pallas_call(
        functools.partial(_inproj_kernel, latent),
        out_shape=out_shape,
        grid=(n_tiles,),
        in_specs=in_specs,
        out_specs=out_specs,
        scratch_shapes=[pltpu.VMEM((TM_IN, W1_COLS), F32)],
        compiler_params=_cparams(("arbitrary",)),
        name="inproj_latent" if latent else "inproj_context",
    )(*args)


def _attend(q, k_parts, v_parts):
    nt = (((1,), (1,)), ((), ()))
    ss = [lax.dot_general(q, k, nt, preferred_element_type=F32) for k in k_parts]
    m = functools.reduce(jnp.maximum, [jnp.max(s, axis=-1, keepdims=True) for s in ss])
    ps = [jnp.exp(s - m) for s in ss]
    l = functools.reduce(jnp.add, [jnp.sum(p, axis=-1, keepdims=True) for p in ps])
    o = functools.reduce(
        jnp.add,
        [jnp.dot(p.astype(BF16), v, preferred_element_type=F32) for p, v in zip(ps, v_parts)])
    return o / l


def _mla_attn_kernel(has_ctx, q_ref, k_ref, v_ref, *rest):
    if has_ctx:
        kc_ref, vc_ref, o_ref = rest
    else:
        (o_ref,) = rest
    lo = _lane_lo(q_ref.shape[1])

    def unit(u, carry):
        outs = []
        for half in range(2):
            hd = 2 * u + half
            ks = ([kc_ref[hd]] if has_ctx else []) + [k_ref[hd]]
            vs = ([vc_ref[u]] if has_ctx else []) + [v_ref[u]]
            outs.append(_attend(q_ref[hd], ks, vs))
        o_ref[u] = jnp.where(lo, outs[0], outs[1]).astype(BF16)
        return carry

    lax.fori_loop(0, MLA_HEADS // 2, unit, 0)


def _gqa_attn_kernel(has_ctx, q_ref, k_ref, v_ref, *rest):
    if has_ctx:
        kc_ref, vc_ref, o_ref = rest
    else:
        (o_ref,) = rest
    lo = _lane_lo(q_ref.shape[1])

    def unit(u, carry):
        kv = u // 2
        ks = ([kc_ref[kv]] if has_ctx else []) + [k_ref[kv]]
        vs = ([vc_ref[kv]] if has_ctx else []) + [v_ref[kv]]
        q_lo, q_hi = _half_heads(q_ref[u], lo)
        o_ref[u] = jnp.where(lo, _attend(q_lo, ks, vs), _attend(q_hi, ks, vs)).astype(BF16)
        return carry

    lax.fori_loop(0, GQA_HEADS // 2, unit, 0)


def _diff_attn_kernel(has_ctx, lam_init, q_ref, k_ref, v_ref, lam_ref, vec_ref, *rest):
    if has_ctx:
        kc_ref, vc_ref, o_ref = rest
    else:
        (o_ref,) = rest
    lo = _lane_lo(q_ref.shape[1])
    lv = lam_ref[...]
    lam = (jnp.exp(jnp.sum(lv[0:1] * lv[1:2], axis=-1, keepdims=True))
           - jnp.exp(jnp.sum(lv[2:3] * lv[3:4], axis=-1, keepdims=True)) + lam_init)
    gain = vec_ref[V_AOUT:V_AOUT + 1, 0:LANES]

    def unit(pp, carry):
        for half in range(2):
            hd = 2 * pp + half
            vs = ([vc_ref[hd]] if has_ctx else []) + [v_ref[hd]]
            os = []
            for m in range(2):
                g = 2 * m + pp
                q = _half_heads(q_ref[g], lo)[half]
                ks = ([kc_ref[g]] if has_ctx else []) + [k_ref[g]]
                os.append(_attend(q, ks, vs))
            o = os[0] - lam * os[1]
            o_ref[hd] = _rms(o, gain).astype(BF16)
        return carry

    lax.fori_loop(0, DIFF_HEADS // 2, unit, 0)


def _attn_specs(latent, groups_q, blk_q, groups_k, blk_k, groups_v, blk_v):
    if latent:
        per = DEC_SEQ // TQ
        q_idx = lambda b, j: (blk_q, b * per + j, 0)
        k_idx = lambda b, j: (blk_k, b, 0)
        v_idx = lambda b, j: (blk_v, b, 0)
        s = DEC_SEQ
    else:
        q_idx = lambda i: (blk_q, i, 0)
        k_idx = lambda i: (blk_k, i, 0)
        v_idx = lambda i: (blk_v, i, 0)
        s = SEQ
    return [
        pl.BlockSpec((groups_q, TQ, LANES), q_idx),
        pl.BlockSpec((groups_k, s, LANES), k_idx),
        pl.BlockSpec((groups_v, s, LANES), v_idx),
    ]


def _ctx_spec(groups, layer):
    return pl.BlockSpec((None, None, groups, PAST_LEN, LANES), lambda b, j: (layer, b, 0, 0, 0))


def _attn_call(kernel, name, latent, specs, args, t):
    if latent:
        grid = (DEC_BATCH, DEC_SEQ // TQ)
        o_idx = lambda b, j: (0, b * (DEC_SEQ // TQ) + j, 0)
        sem = ("arbitrary", "arbitrary")
    else:
        grid = (t // TQ,)
        o_idx = lambda i: (0, i, 0)
        sem = ("arbitrary",)
    return pl.pallas_call(
        kernel,
        out_shape=jax.ShapeDtypeStruct((4, t, LANES), BF16),
        grid=grid,
        in_specs=specs,
        out_specs=pl.BlockSpec((4, TQ, LANES), o_idx),
        compiler_params=_cparams(sem),
        name=name + ("_latent" if latent else "_context"),
    )(*args)


def _attention(qkv, layer, lam_init, lam_l, vec_l, ctx, latent):
    t = qkv.shape[1]
    if latent:
        const2 = lambda b, j: (0, 0)
    else:
        const2 = lambda i: (0, 0)

    specs = _attn_specs(latent, 4, G_AQ // 4, 4, G_AK // 4, 4, G_AV // 4)
    specs += [pl.BlockSpec((4, HEAD_DIM), const2), pl.BlockSpec((N_VEC_ROWS, D_MODEL), const2)]
    args = [qkv, qkv, qkv, lam_l, vec_l]
    if latent:
        specs += [_ctx_spec(4, layer), _ctx_spec(4, layer)]
        args += [ctx["ak"], ctx["av"]]
    oa = _attn_call(functools.partial(_diff_attn_kernel, latent, lam_init), "diff_attn", latent, specs, args, t)

    specs = _attn_specs(latent, 8, G_BQ // 8, 8, G_BK // 8, 4, G_BV // 4)
    args = [qkv, qkv, qkv]
    if latent:
        specs += [_ctx_spec(8, layer), _ctx_spec(4, layer)]
        args += [ctx["bk"], ctx["bv"]]
    ob = _attn_call(functools.partial(_mla_attn_kernel, latent), "mla_attn", latent, specs, args, t)

    specs = _attn_specs(latent, 4, G_CQ // 4, 2, G_CK // 2, 2, G_CV // 2)
    args = [qkv, qkv, qkv]
    if latent:
        specs += [_ctx_spec(2, layer), _ctx_spec(2, layer)]
        args += [ctx["ck"], ctx["cv"]]
    oc = _attn_call(functools.partial(_gqa_attn_kernel, latent), "gqa_attn", latent, specs, args, t)
    return oa, ob, oc


def _merge_kernel(x_ref, mod_ref, vec_ref, oa_ref, ob_ref, oc_ref, wg_ref, wa_ref, wb_ref, wc_ref,
                  wo_ref, out_ref):
    x = x_ref[...]
    h = _rms(x, vec_ref[V_NORM_MIX:V_NORM_MIX + 1, :])
    h = h * (1.0 + mod_ref[M_SCALE_M:M_SCALE_M + 1, :]) + mod_ref[M_SHIFT_M:M_SHIFT_M + 1, :]
    hb = h.astype(BF16)
    merged = None
    for br, (o_ref, w_ref) in enumerate(((oa_ref, wa_ref), (ob_ref, wb_ref), (oc_ref, wc_ref))):
        o = jnp.concatenate([o_ref[g] for g in range(4)], axis=1)
        proj = jnp.dot(o, w_ref[...], preferred_element_type=F32)
        gate = _sigmoid(jnp.dot(hb, wg_ref[:, br * D_MODEL:(br + 1) * D_MODEL],
                                preferred_element_type=F32))
        merged = gate * proj if merged is None else merged + gate * proj
    mix = jnp.dot(merged.astype(BF16), wo_ref[...], preferred_element_type=F32)
    out_ref[...] = x + mod_ref[M_GATE_M:M_GATE_M + 1, :] * mix


def _merge(x, mod_l, vec_l, oa, ob, oc, wg, wa, wb, wc, wo, latent):
    t = x.shape[0]
    per = DEC_SEQ // TM_MERGE
    mod_idx = (lambda i: (1 + i // per, 0, 0)) if latent else (lambda i: (0, 0, 0))
    const2 = lambda i: (0, 0)
    o_spec = pl.BlockSpec((4, TM_MERGE, LANES), lambda i: (0, i, 0))
    width = 4 * LANES
    return pl.pallas_call(
        _merge_kernel,
        out_shape=jax.ShapeDtypeStruct((t, D_MODEL), F32),
        grid=(t // TM_MERGE,),
        in_specs=[
            pl.BlockSpec((TM_MERGE, D_MODEL), lambda i: (i, 0)),
            pl.BlockSpec((None, 6, D_MODEL), mod_idx),
            pl.BlockSpec((N_VEC_ROWS, D_MODEL), const2),
            o_spec, o_spec, o_spec,
            pl.BlockSpec((D_MODEL, 3 * D_MODEL), const2),
            pl.BlockSpec((width, D_MODEL), const2),
            pl.BlockSpec((width, D_MODEL), const2),
            pl.BlockSpec((width, D_MODEL), const2),
            pl.BlockSpec((D_MODEL, D_MODEL), const2),
        ],
        out_specs=pl.BlockSpec((TM_MERGE, D_MODEL), lambda i: (i, 0)),
        compiler_params=_cparams(("arbitrary",)),
        name="merge_latent" if latent else "merge_context",
    )(x, mod_l, vec_l, oa, ob, oc, wg, wa, wb, wc, wo)


def _ffn_kernel(seq_len, x_ref, mod_ref, vec_ref, wua_ref, wug_ref, cwa_ref, cwg_ref, cba_ref, cbg_ref,
                wd_ref, out_ref, h_scr):
    k = pl.program_id(1)
    rows = x_ref.shape[0]

    @pl.when(k == 0)
    def _():
        h = _rms(x_ref[...], vec_ref[V_NORM_FFN:V_NORM_FFN + 1, :])
        h = h * (1.0 + mod_ref[M_SCALE_F:M_SCALE_F + 1, :]) + mod_ref[M_SHIFT_F:M_SHIFT_F + 1, :]
        h_scr[...] = h.astype(BF16)
        out_ref[...] = jnp.zeros_like(out_ref)

    hb = h_scr[...]
    pos = lax.broadcasted_iota(jnp.int32, (rows, TF), 0) & (seq_len - 1)
    first = pos == 0
    last = pos == seq_len - 1

    def branch(w_ref, cw_ref, cb_ref):
        u = jnp.dot(hb, w_ref[...], preferred_element_type=F32)
        prev = jnp.where(first, 0.0, pltpu.roll(u, 1, 0))
        nxt = jnp.where(last, 0.0, pltpu.roll(u, rows - 1, 0))
        return prev * cw_ref[0:1, :] + u * cw_ref[1:2, :] + nxt * cw_ref[2:3, :] + cb_ref[...]

    a = branch(wua_ref, cwa_ref, cba_ref)
    g = branch(wug_ref, cwg_ref, cbg_ref)
    act = (a * (g * _sigmoid(g))).astype(BF16)
    out_ref[...] += jnp.dot(act, wd_ref[...], preferred_element_type=F32)

    @pl.when(k == pl.num_programs(1) - 1)
    def _():
        out_ref[...] = x_ref[...] + mod_ref[M_GATE_F:M_GATE_F + 1, :] * out_ref[...]


def _ffn(x, mod_l, vec_l, w_up, conv_w, conv_b, w_down, latent):
    t = x.shape[0]
    nf = D_FF // TF
    per = DEC_SEQ // TM_FFN
    mod_idx = (lambda i, k: (1 + i // per, 0, 0)) if latent else (lambda i, k: (0, 0, 0))
    seq_len = DEC_SEQ if latent else SEQ
    return pl.pallas_call(
        functools.partial(_ffn_kernel, seq_len),
        out_shape=jax.ShapeDtypeStruct((t, D_MODEL), F32),
        grid=(t // TM_FFN, nf),
        in_specs=[
            pl.BlockSpec((TM_FFN, D_MODEL), lambda i, k: (i, 0), pipeline_mode=pl.Buffered(1)),
            pl.BlockSpec((None, 6, D_MODEL), mod_idx),
            pl.BlockSpec((N_VEC_ROWS, D_MODEL), lambda i, k: (0, 0)),
            pl.BlockSpec((D_MODEL, TF), lambda i, k: (0, k)),
            pl.BlockSpec((D_MODEL, TF), lambda i, k: (0, nf + k)),
            pl.BlockSpec((3, TF), lambda i, k: (0, k)),
            pl.BlockSpec((3, TF), lambda i, k: (0, nf + k)),
            pl.BlockSpec((1, TF), lambda i, k: (0, k)),
            pl.BlockSpec((1, TF), lambda i, k: (0, nf + k)),
            pl.BlockSpec((TF, D_MODEL), lambda i, k: (k, 0)),
        ],
        out_specs=pl.BlockSpec((TM_FFN, D_MODEL), lambda i, k: (i, 0)),
        scratch_shapes=[pltpu.VMEM((TM_FFN, D_MODEL), BF16)],
        compiler_params=_cparams(("arbitrary", "arbitrary")),
        name="ffn_latent" if latent else "ffn_context",
    )(x, mod_l, vec_l, w_up, w_up, conv_w, conv_w, conv_b, conv_b, w_down)


def _rope_angles(n_tokens, dim):
    rows = n_tokens // GRID_W
    row = jnp.repeat(jnp.arange(rows, dtype=F32), GRID_W)
    col = jnp.tile(jnp.arange(GRID_W, dtype=F32), rows)
    quarter = dim // 4
    inv = 1.0 / (ROPE_BASE ** (jnp.arange(quarter, dtype=F32) / quarter))
    ang = jnp.concatenate([row[:, None] * inv, col[:, None] * inv], axis=-1)
    return jnp.cos(ang), jnp.sin(ang)


def _rope_tables():
    t = DEC_SEQ
    cos_h, sin_h = _rope_angles(t, HEAD_DIM)
    zeros = jnp.zeros_like(sin_h)
    c64 = jnp.tile(cos_h, (1, 4))
    up64 = jnp.tile(jnp.concatenate([-sin_h, zeros], axis=1), (1, 2))
    dn64 = jnp.tile(jnp.concatenate([zeros, sin_h], axis=1), (1, 2))
    cos_r, sin_r = _rope_angles(t, MLA_ROPE_DIM)
    z16 = jnp.zeros_like(sin_r)
    ones64 = jnp.ones((t, MLA_NOPE_DIM), F32)
    z64 = jnp.zeros((t, MLA_NOPE_DIM), F32)
    pad = LANES - MLA_QK_DIM
    cm = jnp.concatenate([ones64, cos_r, cos_r, jnp.ones((t, pad), F32)], axis=1)
    upm = jnp.concatenate([z64, -sin_r, z16, jnp.zeros((t, pad), F32)], axis=1)
    dnm = jnp.concatenate([z64, z16, sin_r, jnp.zeros((t, pad), F32)], axis=1)
    return jnp.stack([c64, up64, dn64, cm, upm, dnm])


def _pad_heads96(v):
    v = jnp.broadcast_to(v.reshape(-1, MLA_QK_DIM), (MLA_HEADS, MLA_QK_DIM))
    return jnp.pad(v, ((0, 0), (0, LANES - MLA_QK_DIM))).reshape(-1)


def _vec_pack(l, lam_init, norm_mix, norm_ffn, diff_qk_norm, mla_q_norm, mla_kv_norm, mla_qk_norm,
              gqa_qk_norm, diff_out_norm):
    def row(v):
        return jnp.pad(v.astype(F32), (0, D_MODEL - v.shape[0]))
    rows = [None] * N_VEC_ROWS
    rows[V_NORM_MIX] = row(norm_mix[l])
    rows[V_NORM_FFN] = row(norm_ffn[l])
    rows[V_AQ] = row(jnp.tile(diff_qk_norm[l, 0], 8) * HEAD_DIM ** -0.5)
    rows[V_AK] = row(jnp.tile(diff_qk_norm[l, 1], 8))
    rows[V_BQD] = row(mla_q_norm[l])
    rows[V_BQ] = row(_pad_heads96(mla_qk_norm[l, 0]) * MLA_QK_DIM ** -0.5)
    rows[V_CKV] = row(mla_kv_norm[l])
    rows[V_BK] = row(_pad_heads96(mla_qk_norm[l, 1]))
    rows[V_CQ] = row(jnp.tile(gqa_qk_norm[l, 0], 8) * HEAD_DIM ** -0.5)
    rows[V_CK] = row(jnp.tile(gqa_qk_norm[l, 1], 4))
    rows[V_AOUT] = row(diff_out_norm[l] * (1.0 - lam_init))
    zero = jnp.zeros((D_MODEL,), F32)
    return jnp.stack([r if r is not None else zero for r in rows])


def _dup_heads64(w, n_heads):
    lead = w.shape[:-1]
    w = w.reshape(lead + (n_heads, 1, HEAD_DIM))
    return jnp.broadcast_to(w, lead + (n_heads, 2, HEAD_DIM)).reshape(lead + (n_heads * LANES,))


def _w1_layout(w_in_l):
    d = D_MODEL
    aq, ak, av = w_in_l[:, 0:512], w_in_l[:, 512:1024], w_in_l[:, 1024:1536]
    bqd = w_in_l[:, 1536:1792]
    ckv = w_in_l[:, 1792:1920]
    kpe = w_in_l[:, 1920:1952]
    cq = w_in_l[:, 1952:2464]
    ck = w_in_l[:, 2464:2592]
    cv = w_in_l[:, 2592:2720]
    gates = w_in_l[:, 2720:]
    kpe_grp = jnp.concatenate(
        [jnp.zeros((d, MLA_NOPE_DIM), F32), kpe, jnp.zeros((d, LANES - MLA_QK_DIM), F32)], axis=1)
    w1 = jnp.concatenate(
        [aq, ak, av, bqd, ckv, kpe_grp, cq, _dup_heads64(ck, GQA_KV_HEADS), _dup_heads64(cv, GQA_KV_HEADS)],
        axis=1)
    return w1.astype(BF16), gates.astype(BF16)


def _wq_layout(w_q_up_l):
    w = w_q_up_l.reshape(MLA_Q_RANK, MLA_HEADS, MLA_QK_DIM)
    w = jnp.pad(w, ((0, 0), (0, 0), (0, LANES - MLA_QK_DIM)))
    return w.reshape(MLA_Q_RANK, MLA_HEADS * LANES).astype(BF16)


def _wkv_layout(w_kv_up_l):
    w = w_kv_up_l.reshape(MLA_KV_RANK, MLA_HEADS, MLA_NOPE_DIM + MLA_V_DIM)
    k = jnp.pad(w[:, :, :MLA_NOPE_DIM], ((0, 0), (0, 0), (0, LANES - MLA_NOPE_DIM)))
    v = w[:, :, MLA_NOPE_DIM:]
    return jnp.concatenate(
        [k.reshape(MLA_KV_RANK, MLA_HEADS * LANES), v.reshape(MLA_KV_RANK, MLA_HEADS * MLA_V_DIM)],
        axis=1).astype(BF16)


def _groups(x, n_groups):
    b, l, s, _ = x.shape
    return jnp.transpose(x.reshape(b, l, s, n_groups, LANES), (1, 0, 3, 2, 4)).astype(BF16)


def kernel(x_prompt, x_sample, c, cache_diff_k, cache_diff_v, cache_mla_ckv, cache_mla_kpe,
           cache_gqa_k, cache_gqa_v, c_ctx, w_ada, b_ada, norm_mix, w_in, diff_qk_norm,
           diff_lambda, diff_out_norm, mla_q_norm, w_mla_q_up, mla_kv_norm, w_mla_kv_up,
           mla_qk_norm, gqa_qk_norm, w_branch_a, w_branch_b, w_branch_c, w_o, norm_ffn,
           w_up, conv_w, conv_b, w_down):
    lam_inits = [0.8 - 0.6 * math.exp(-0.3 * l) for l in range(DEPTH)]
    vecs = jnp.stack([
        _vec_pack(l, lam_inits[l], norm_mix, norm_ffn, diff_qk_norm, mla_q_norm, mla_kv_norm,
                  mla_qk_norm, gqa_qk_norm, diff_out_norm) for l in range(DEPTH)])
    wkv_all = jnp.stack([_wkv_layout(w_mla_kv_up[l]) for l in range(DEPTH)])

    cvec = jnp.concatenate(
        [c_ctx[None, :], c, jnp.zeros((N_MOD_ROWS - 1 - DEC_BATCH, D_MODEL), F32)], axis=0)
    mods = _mods(cvec, w_ada, b_ada).reshape(DEPTH, N_MOD_ROWS, 6, D_MODEL)

    rope_tab = _rope_tables()

    kpe_grp = jnp.pad(cache_mla_kpe, ((0, 0), (0, 0), (0, 0), (MLA_NOPE_DIM, LANES - MLA_QK_DIM)))
    ctx_bk, ctx_bv = _ctx_mla(cache_mla_ckv, kpe_grp, wkv_all, vecs)
    nb, nl, ns = DEC_BATCH, DEPTH, PAST_LEN
    ctx = {
        "ak": _groups(cache_diff_k.reshape(nb, nl, ns, 512), 4),
        "av": _groups(cache_diff_v.reshape(nb, nl, ns, 512), 4),
        "bk": ctx_bk,
        "bv": ctx_bv,
        "ck": _groups(_dup_heads64(cache_gqa_k.reshape(nb, nl, ns, 128), GQA_KV_HEADS), 2),
        "cv": _groups(_dup_heads64(cache_gqa_v.reshape(nb, nl, ns, 128), GQA_KV_HEADS), 2),
    }

    y_p = x_prompt.reshape(BATCH * SEQ, D_MODEL)
    y_s = x_sample.reshape(DEC_BATCH * DEC_SEQ, D_MODEL)
    states = []
    for l in range(DEPTH):
        w1, wg = _w1_layout(w_in[l])
        wq = _wq_layout(w_mla_q_up[l])
        wkv = wkv_all[l]
        wa, wb, wc = (w.astype(BF16) for w in (w_branch_a[l], w_branch_b[l], w_branch_c[l]))
        wo = w_o[l].astype(BF16)
        wu, wd = w_up[l].astype(BF16), w_down[l].astype(BF16)
        cw, cb = conv_w[l], conv_b[l].reshape(1, 2 * D_FF)
        vec_l, mod_l, lam_l = vecs[l], mods[l], diff_lambda[l]

        qkv, st_ak, st_av, st_ckv, st_kpe, st_ck, st_cv = _inproj(
            y_p, mod_l, vec_l, w1, wq, wkv, None, latent=False)
        states.append((st_ak, st_av, st_ckv, st_kpe[:, MLA_NOPE_DIM:MLA_QK_DIM], st_ck, st_cv))
        oa, ob, oc = _attention(qkv, l, lam_inits[l], lam_l, vec_l, None, latent=False)
        y_p = _merge(y_p, mod_l, vec_l, oa, ob, oc, wg, wa, wb, wc, wo, latent=False)
        y_p = _ffn(y_p, mod_l, vec_l, wu, cw, cb, wd, latent=False)

        qkv = _inproj(y_s, mod_l, vec_l, w1, wq, wkv, rope_tab, latent=True)
        oa, ob, oc = _attention(qkv, l, lam_inits[l], lam_l, vec_l, ctx, latent=True)
        y_s = _merge(y_s, mod_l, vec_l, oa, ob, oc, wg, wa, wb, wc, wo, latent=True)
        y_s = _ffn(y_s, mod_l, vec_l, wu, cw, cb, wd, latent=True)

    def stacked(i, shape):
        return jnp.stack([s[i].reshape((BATCH, SEQ) + shape) for s in states], axis=1)

    return (
        y_p.reshape(BATCH, SEQ, D_MODEL),
        y_s.reshape(DEC_BATCH, DEC_SEQ, D_MODEL),
        stacked(0, (2, DIFF_HEADS, HEAD_DIM)),
        stacked(1, (DIFF_HEADS, DIFF_V_DIM)),
        stacked(2, (MLA_KV_RANK,)),
        stacked(3, (MLA_ROPE_DIM,)),
        stacked(4, (GQA_KV_HEADS, HEAD_DIM)),
        stacked(5, (GQA_KV_HEADS, HEAD_DIM)),
    )
```

```python
import functools
import math

import jax
import jax.numpy as jnp
import numpy as np
from jax import lax
from jax.experimental import pallas as pl
from jax.experimental.pallas import tpu as pltpu

D_MODEL = 1024
BATCH = 32
SEQ = 256
DEPTH = 4
DEC_BATCH = 4
DEC_SEQ = 2048
PAST_LEN = 256
GRID_W = 64
HEAD_DIM = 64
DIFF_HEADS = 4
DIFF_V_DIM = 2 * HEAD_DIM
MLA_HEADS = 8
MLA_Q_RANK = 256
MLA_KV_RANK = 128
MLA_NOPE_DIM = 64
MLA_ROPE_DIM = 32
MLA_QK_DIM = MLA_NOPE_DIM + MLA_ROPE_DIM
MLA_V_DIM = 64
GQA_HEADS = 8
GQA_KV_HEADS = 2
D_FF = 2816
ROPE_BASE = 10000.0
NORM_EPS = 1e-6
LOG2E = math.log2(math.e)

LANES = 128
MXU_N = 256
VMEM_LIMIT_BYTES = 56 * 1024 * 1024

F32 = jnp.float32
BF16 = jnp.bfloat16

C_AQ, C_AK, C_AV = 0, 512, 1024
C_BQD, C_CKV, C_KPE = 1536, 1792, 1920
C_CQ, C_CK, C_CV = 2048, 2560, 2816
W1_COLS = 3072
IN_GATES_COL = 2720

G_BQ, G_BK, G_AQ, G_AK, G_AV, G_BV, G_CQ, G_CK, G_CV = 0, 8, 16, 20, 24, 28, 32, 36, 38
N_GROUPS = 40

V_NORM_MIX, V_NORM_FFN, V_AQ, V_AK, V_BQD, V_BQ, V_CKV, V_BK, V_CQ, V_CK, V_AOUT = range(11)
N_VEC_ROWS = 16

M_SHIFT_M, M_SCALE_M, M_GATE_M, M_SHIFT_F, M_SCALE_F, M_GATE_F = range(6)
N_MOD_ROWS = 8

TM_IN = 256
TQ = 256
CTX_SEQS_PER_STEP = 2
TM_MERGE = 512
TM_FFN = 1024
FFN_CONV_ROWS = 128
FFN_HALO = 16
TF = 256
N_FF_CHUNKS = D_FF // TF
FFN_SLOTS = 3
TN_ADA = 1536


def _cparams(sem, flags=None):
    return pltpu.CompilerParams(dimension_semantics=sem, vmem_limit_bytes=VMEM_LIMIT_BYTES, flags=flags)


def _rms(x, gain):
    ms = jnp.mean(x * x, axis=-1, keepdims=True)
    return x * lax.rsqrt(ms + NORM_EPS) * gain


def _lane_lo(rows):
    return lax.broadcasted_iota(jnp.int32, (rows, LANES), 1) < HEAD_DIM


def _first_head_lanes(rows, interleaved):
    lane = lax.broadcasted_iota(jnp.int32, (rows, LANES), 1)
    if interleaved:
        return (lane & (HEAD_DIM // 2)) == 0
    return lane < HEAD_DIM


def _norm_heads64(xg, gain_g, lo):
    sq = xg * xg
    s_all = jnp.sum(sq, axis=-1, keepdims=True)
    s_lo = jnp.sum(jnp.where(lo, sq, 0.0), axis=-1, keepdims=True)
    ssq = jnp.where(lo, s_lo, s_all - s_lo)
    return xg * lax.rsqrt(ssq * (1.0 / HEAD_DIM) + NORM_EPS) * gain_g


def _norm_head96(xg, gain_g):
    ssq = jnp.sum(xg * xg, axis=-1, keepdims=True)
    return xg * lax.rsqrt(ssq * (1.0 / MLA_QK_DIM) + NORM_EPS) * gain_g


def _rope(xg, c, s):
    return xg * c + pltpu.roll(xg, LANES // 2, 1) * s


def _sigmoid(z):
    return 1.0 / (1.0 + jnp.exp(-z))


def _half_heads(qg, lo):
    qf = qg.astype(F32)
    return jnp.where(lo, qf, 0.0).astype(BF16), jnp.where(lo, 0.0, qf).astype(BF16)


def _mods_kernel(c_ref, w_ref, b_ref, o_ref):
    c = c_ref[...]
    a = c * _sigmoid(c)
    w = w_ref[...]
    a_hi = a.astype(BF16)
    a_lo = (a - a_hi.astype(F32)).astype(BF16)
    w_hi = w.astype(BF16)
    w_lo = (w - w_hi.astype(F32)).astype(BF16)
    acc = jnp.dot(a_hi, w_hi, preferred_element_type=F32)
    acc = acc + jnp.dot(a_lo, w_hi, preferred_element_type=F32)
    acc = acc + jnp.dot(a_hi, w_lo, preferred_element_type=F32)
    o_ref[...] = acc + b_ref[...]


def _mods(cvec, w_ada, b_ada):
    n = 6 * D_MODEL
    return pl.pallas_call(
        _mods_kernel,
        out_shape=jax.ShapeDtypeStruct((DEPTH, N_MOD_ROWS, n), F32),
        grid=(DEPTH, n // TN_ADA),
        in_specs=[
            pl.BlockSpec((N_MOD_ROWS, D_MODEL), lambda l, j: (0, 0)),
            pl.BlockSpec((None, D_MODEL, TN_ADA), lambda l, j: (l, 0, j)),
            pl.BlockSpec((None, 1, TN_ADA), lambda l, j: (l, 0, j)),
        ],
        out_specs=pl.BlockSpec((None, N_MOD_ROWS, TN_ADA), lambda l, j: (l, 0, j)),
        compiler_params=_cparams(("arbitrary", "arbitrary")),
        name="ada_mods",
    )(cvec, w_ada, b_ada.reshape(DEPTH, 1, n))


def _mla_expand(ckv, kpe_grp, wkv_ref, vec_ref):
    ckvn = _rms(ckv, vec_ref[V_CKV:V_CKV + 1, 0:MLA_KV_RANK]).astype(BF16)
    kv = jnp.dot(ckvn, wkv_ref[...], preferred_element_type=F32)
    keys = []
    for h in range(MLA_HEADS):
        kpre = kv[:, h * LANES:(h + 1) * LANES] + kpe_grp
        keys.append(_norm_head96(kpre, vec_ref[V_BK:V_BK + 1, h * LANES:(h + 1) * LANES]))
    return keys, kv[:, MLA_HEADS * LANES:]


def _ctx_mla_kernel(ckv_ref, kpe_ref, wkv_ref, vec_ref, k_ref, v_ref):
    keys, vals = _mla_expand(ckv_ref[...], kpe_ref[...], wkv_ref, vec_ref)
    for h in range(MLA_HEADS):
        k_ref[h] = keys[h].astype(BF16)
    for p in range(MLA_HEADS // 2):
        v_ref[p] = vals[:, p * LANES:(p + 1) * LANES].astype(BF16)


def _ctx_mla(cache_ckv, cache_kpe_grp, wkv, vecs):
    return pl.pallas_call(
        _ctx_mla_kernel,
        out_shape=(
            jax.ShapeDtypeStruct((DEPTH, DEC_BATCH, MLA_HEADS, PAST_LEN, LANES), BF16),
            jax.ShapeDtypeStruct((DEPTH, DEC_BATCH, MLA_HEADS // 2, PAST_LEN, LANES), BF16),
        ),
        grid=(DEPTH, DEC_BATCH),
        in_specs=[
            pl.BlockSpec((None, None, PAST_LEN, MLA_KV_RANK), lambda l, b: (b, l, 0, 0)),
            pl.BlockSpec((None, None, PAST_LEN, LANES), lambda l, b: (b, l, 0, 0)),
            pl.BlockSpec((None, MLA_KV_RANK, 12 * LANES), lambda l, b: (l, 0, 0)),
            pl.BlockSpec((None, N_VEC_ROWS, D_MODEL), lambda l, b: (l, 0, 0)),
        ],
        out_specs=(
            pl.BlockSpec((None, None, MLA_HEADS, PAST_LEN, LANES), lambda l, b: (l, b, 0, 0, 0)),
            pl.BlockSpec((None, None, MLA_HEADS // 2, PAST_LEN, LANES), lambda l, b: (l, b, 0, 0, 0)),
        ),
        compiler_params=_cparams(("arbitrary", "arbitrary")),
        name="ctx_mla_expand",
    )(cache_ckv, cache_kpe_grp, wkv, vecs)


def _inproj_kernel(latent, x_ref, mod_ref, vec_ref, w1_ref, wq_ref, wkv_ref, *rest):
    if latent:
        rope_ref, qkv_ref, p_scr = rest
    else:
        qkv_ref, st_ak, st_av, st_ckv, st_kpe, st_ck, st_cv, p_scr = rest
    rows = x_ref.shape[0]
    lo = _first_head_lanes(rows, interleaved=latent)

    x = x_ref[...]
    h = _rms(x, vec_ref[V_NORM_MIX:V_NORM_MIX + 1, :])
    h = h * (1.0 + mod_ref[M_SCALE_M:M_SCALE_M + 1, :]) + mod_ref[M_SHIFT_M:M_SHIFT_M + 1, :]
    p_scr[...] = jnp.dot(h.astype(BF16), w1_ref[...], preferred_element_type=F32)

    def grp(col, g):
        return p_scr[:, col + g * LANES:col + (g + 1) * LANES]

    def vec(row, g):
        return vec_ref[row:row + 1, g * LANES:(g + 1) * LANES]

    def rope64(y):
        if not latent:
            return y
        return _rope(y, rope_ref[0], rope_ref[1])

    def rope_tail(y):
        if not latent:
            return y
        return _rope(y, rope_ref[2], rope_ref[3])

    for g in range(4):
        qkv_ref[G_AQ + g] = rope64(_norm_heads64(grp(C_AQ, g), vec(V_AQ, g), lo)).astype(BF16)
        akn = _norm_heads64(grp(C_AK, g), vec(V_AK, g), lo)
        qkv_ref[G_AK + g] = rope64(akn).astype(BF16)
        av = grp(C_AV, g)
        qkv_ref[G_AV + g] = av.astype(BF16)
        if not latent:
            st_ak[:, g * LANES:(g + 1) * LANES] = akn
            st_av[:, g * LANES:(g + 1) * LANES] = av

    bqn = _rms(p_scr[:, C_BQD:C_BQD + MLA_Q_RANK], vec_ref[V_BQD:V_BQD + 1, 0:MLA_Q_RANK]).astype(BF16)
    bq = jnp.dot(bqn, wq_ref[...], preferred_element_type=F32)
    for hd in range(MLA_HEADS):
        qn = _norm_head96(bq[:, hd * LANES:(hd + 1) * LANES], vec(V_BQ, hd))
        qkv_ref[G_BQ + hd] = rope_tail(qn).astype(BF16)
    ckv = p_scr[:, C_CKV:C_CKV + MLA_KV_RANK]
    kpe_grp = p_scr[:, C_KPE:C_KPE + LANES]
    keys, vals = _mla_expand(ckv, kpe_grp, wkv_ref, vec_ref)
    for hd in range(MLA_HEADS):
        qkv_ref[G_BK + hd] = rope_tail(keys[hd]).astype(BF16)
    for p in range(MLA_HEADS // 2):
        qkv_ref[G_BV + p] = vals[:, p * LANES:(p + 1) * LANES].astype(BF16)
    if not latent:
        st_ckv[...] = ckv
        st_kpe[...] = kpe_grp

    for g in range(4):
        qkv_ref[G_CQ + g] = rope64(_norm_heads64(grp(C_CQ, g), vec(V_CQ, g), lo)).astype(BF16)
    ckn = [_norm_heads64(grp(C_CK, g), vec(V_CK, g), lo) for g in range(2)]
    cvs = [grp(C_CV, g) for g in range(2)]
    for g in range(2):
        qkv_ref[G_CK + g] = rope64(ckn[g]).astype(BF16)
        qkv_ref[G_CV + g] = cvs[g].astype(BF16)
    if not latent:
        st_ck[...] = jnp.where(lo, ckn[0], ckn[1])
        st_cv[...] = jnp.where(lo, cvs[0], cvs[1])


def _inproj(x, mod_l, vec_l, w1, wq, wkv, rope_tab, latent):
    t = x.shape[0]
    n_tiles = t // TM_IN
    tiles_per_seq = DEC_SEQ // TM_IN
    if latent:
        mod_idx = lambda i: (1 + i // tiles_per_seq, 0, 0)
    else:
        mod_idx = lambda i: (0, 0, 0)
    const2 = lambda i: (0, 0)
    in_specs = [
        pl.BlockSpec((TM_IN, D_MODEL), lambda i: (i, 0)),
        pl.BlockSpec((None, 6, D_MODEL), mod_idx),
        pl.BlockSpec((N_VEC_ROWS, D_MODEL), const2),
        pl.BlockSpec((D_MODEL, W1_COLS), const2),
        pl.BlockSpec((MLA_Q_RANK, MLA_HEADS * LANES), const2),
        pl.BlockSpec((MLA_KV_RANK, 12 * LANES), const2),
    ]
    args = [x, mod_l, vec_l, w1, wq, wkv]
    qkv_shape = jax.ShapeDtypeStruct((N_GROUPS, t, LANES), BF16)
    qkv_spec = pl.BlockSpec((N_GROUPS, TM_IN, LANES), lambda i: (0, i, 0))
    if latent:
        in_specs.append(pl.BlockSpec((4, TM_IN, LANES), lambda i: (0, i % tiles_per_seq, 0)))
        args.append(rope_tab)
        out_shape = qkv_shape
        out_specs = qkv_spec
    else:
        widths = (512, 512, LANES, LANES, LANES, LANES)
        out_shape = (qkv_shape,) + tuple(jax.ShapeDtypeStruct((t, w), F32) for w in widths)
        out_specs = (qkv_spec,) + tuple(pl.BlockSpec((TM_IN, w), lambda i: (i, 0)) for w in widths)
    return pl.pallas_call(
        functools.partial(_inproj_kernel, latent),
        out_shape=out_shape,
        grid=(n_tiles,),
        in_specs=in_specs,
        out_specs=out_specs,
        scratch_shapes=[pltpu.VMEM((TM_IN, W1_COLS), F32)],
        compiler_params=_cparams(("arbitrary",)),
        name="inproj_latent" if latent else "inproj_context",
    )(*args)


def _attend(q, k_parts, v_parts):
    nt = (((1,), (1,)), ((), ()))
    ss = [lax.dot_general(q, k, nt, preferred_element_type=F32) for k in k_parts]
    m = functools.reduce(jnp.maximum, [jnp.max(s, axis=-1, keepdims=True) for s in ss])
    ps = [jnp.exp2(s - m) for s in ss]
    l = functools.reduce(jnp.add, [jnp.sum(p, axis=-1, keepdims=True) for p in ps])
    o = functools.reduce(
        jnp.add,
        [jnp.dot(p.astype(BF16), v, preferred_element_type=F32) for p, v in zip(ps, v_parts)])
    return o / l


def _seq_rows(ref, n_seq, sq):
    n = ref.shape[1] // n_seq
    return pl.ds(sq * n, n)


def _mla_attn_kernel(has_ctx, n_seq, q_ref, k_ref, v_ref, *rest):
    if has_ctx:
        kc_ref, vc_ref, o_ref = rest
    else:
        (o_ref,) = rest
    lo_out = _lane_lo(q_ref.shape[1] // n_seq)
    for sq in range(n_seq):
        rq, rk = _seq_rows(q_ref, n_seq, sq), _seq_rows(k_ref, n_seq, sq)
        for u in range(MLA_HEADS // 2):
            outs = []
            for half in range(2):
                hd = 2 * u + half
                ks = ([kc_ref[hd]] if has_ctx else []) + [k_ref[hd, rk, :]]
                vs = ([vc_ref[u]] if has_ctx else []) + [v_ref[u, rk, :]]
                outs.append(_attend(q_ref[hd, rq, :], ks, vs))
            o_ref[u, rq, :] = jnp.where(lo_out, outs[0], outs[1]).astype(BF16)


def _gqa_attn_kernel(has_ctx, n_seq, q_ref, k_ref, v_ref, *rest):
    if has_ctx:
        kc_ref, vc_ref, o_ref = rest
    else:
        (o_ref,) = rest
    tq = q_ref.shape[1] // n_seq
    lo_out = _lane_lo(tq)
    lo_q = _first_head_lanes(tq, interleaved=has_ctx)
    for sq in range(n_seq):
        rq, rk = _seq_rows(q_ref, n_seq, sq), _seq_rows(k_ref, n_seq, sq)
        for u in range(GQA_HEADS // 2):
            kv = u // 2
            ks = ([kc_ref[kv]] if has_ctx else []) + [k_ref[kv, rk, :]]
            vs = ([vc_ref[kv]] if has_ctx else []) + [v_ref[kv, rk, :]]
            q_a, q_b = _half_heads(q_ref[u, rq, :], lo_q)
            o_ref[u, rq, :] = jnp.where(lo_out, _attend(q_a, ks, vs), _attend(q_b, ks, vs)).astype(BF16)


def _diff_attn_kernel(has_ctx, n_seq, lam_init, q_ref, k_ref, v_ref, lam_ref, vec_ref, *rest):
    if has_ctx:
        kc_ref, vc_ref, o_ref = rest
    else:
        (o_ref,) = rest
    lo_q = _first_head_lanes(q_ref.shape[1] // n_seq, interleaved=has_ctx)
    lv = lam_ref[...]
    lam = (jnp.exp(jnp.sum(lv[0:1] * lv[1:2], axis=-1, keepdims=True))
           - jnp.exp(jnp.sum(lv[2:3] * lv[3:4], axis=-1, keepdims=True)) + lam_init)
    gain = vec_ref[V_AOUT:V_AOUT + 1, 0:LANES]
    for sq in range(n_seq):
        rq, rk = _seq_rows(q_ref, n_seq, sq), _seq_rows(k_ref, n_seq, sq)
        for hd in range(DIFF_HEADS):
            pp, half = hd // 2, hd % 2
            vs = ([vc_ref[hd]] if has_ctx else []) + [v_ref[hd, rk, :]]
            os = []
            for m in range(2):
                g = 2 * m + pp
                q = _half_heads(q_ref[g, rq, :], lo_q)[half]
                ks = ([kc_ref[g]] if has_ctx else []) + [k_ref[g, rk, :]]
                os.append(_attend(q, ks, vs))
            o_ref[hd, rq, :] = _rms(os[0] - lam * os[1], gain).astype(BF16)


def _attn_specs(latent, groups_q, blk_q, groups_k, blk_k, groups_v, blk_v):
    if latent:
        per = DEC_SEQ // TQ
        q_idx = lambda b, j: (blk_q, b * per + j, 0)
        k_idx = lambda b, j: (blk_k, b, 0)
        v_idx = lambda b, j: (blk_v, b, 0)
        rows_q, rows_k = TQ, DEC_SEQ
    else:
        q_idx = lambda i: (blk_q, i, 0)
        k_idx = lambda i: (blk_k, i, 0)
        v_idx = lambda i: (blk_v, i, 0)
        rows_q = rows_k = CTX_SEQS_PER_STEP * SEQ
    return [
        pl.BlockSpec((groups_q, rows_q, LANES), q_idx),
        pl.BlockSpec((groups_k, rows_k, LANES), k_idx),
        pl.BlockSpec((groups_v, rows_k, LANES), v_idx),
    ]


def _ctx_spec(groups, layer):
    return pl.BlockSpec((None, None, groups, PAST_LEN, LANES), lambda b, j: (layer, b, 0, 0, 0))


def _attn_call(kernel, name, latent, specs, args, t):
    if latent:
        grid = (DEC_BATCH, DEC_SEQ // TQ)
        o_idx = lambda b, j: (0, b * (DEC_SEQ // TQ) + j, 0)
        sem = ("arbitrary", "arbitrary")
        rows = TQ
    else:
        rows = CTX_SEQS_PER_STEP * SEQ
        grid = (t // rows,)
        o_idx = lambda i: (0, i, 0)
        sem = ("arbitrary",)
    return pl.pallas_call(
        kernel,
        out_shape=jax.ShapeDtypeStruct((4, t, LANES), BF16),
        grid=grid,
        in_specs=specs,
        out_specs=pl.BlockSpec((4, rows, LANES), o_idx),
        compiler_params=_cparams(sem),
        name=name + ("_latent" if latent else "_context"),
    )(*args)


def _attention(qkv, layer, lam_init, lam_l, vec_l, ctx, latent):
    t = qkv.shape[1]
    n_seq = 1 if latent else CTX_SEQS_PER_STEP
    if latent:
        const2 = lambda b, j: (0, 0)
    else:
        const2 = lambda i: (0, 0)

    specs = _attn_specs(latent, 4, G_AQ // 4, 4, G_AK // 4, 4, G_AV // 4)
    specs += [pl.BlockSpec((4, HEAD_DIM), const2), pl.BlockSpec((N_VEC_ROWS, D_MODEL), const2)]
    args = [qkv, qkv, qkv, lam_l, vec_l]
    if latent:
        specs += [_ctx_spec(4, layer), _ctx_spec(4, layer)]
        args += [ctx["ak"], ctx["av"]]
    oa = _attn_call(functools.partial(_diff_attn_kernel, latent, n_seq, lam_init), "diff_attn", latent,
                    specs, args, t)

    specs = _attn_specs(latent, 8, G_BQ // 8, 8, G_BK // 8, 4, G_BV // 4)
    args = [qkv, qkv, qkv]
    if latent:
        specs += [_ctx_spec(8, layer), _ctx_spec(4, layer)]
        args += [ctx["bk"], ctx["bv"]]
    ob = _attn_call(functools.partial(_mla_attn_kernel, latent, n_seq), "mla_attn", latent, specs, args, t)

    specs = _attn_specs(latent, 4, G_CQ // 4, 2, G_CK // 2, 2, G_CV // 2)
    args = [qkv, qkv, qkv]
    if latent:
        specs += [_ctx_spec(2, layer), _ctx_spec(2, layer)]
        args += [ctx["ck"], ctx["cv"]]
    oc = _attn_call(functools.partial(_gqa_attn_kernel, latent, n_seq), "gqa_attn", latent, specs, args, t)
    return oa, ob, oc


def _merge_kernel(x_ref, mod_ref, vec_ref, oa_ref, ob_ref, oc_ref, wg_ref, wa_ref, wb_ref, wc_ref,
                  wo_ref, out_ref):
    x = x_ref[...]
    h = _rms(x, vec_ref[V_NORM_MIX:V_NORM_MIX + 1, :])
    h = h * (1.0 + mod_ref[M_SCALE_M:M_SCALE_M + 1, :]) + mod_ref[M_SHIFT_M:M_SHIFT_M + 1, :]
    hb = h.astype(BF16)
    merged = None
    for br, (o_ref, w_ref) in enumerate(((oa_ref, wa_ref), (ob_ref, wb_ref), (oc_ref, wc_ref))):
        o = jnp.concatenate([o_ref[g] for g in range(4)], axis=1)
        proj = jnp.dot(o, w_ref[...], preferred_element_type=F32)
        gate = _sigmoid(jnp.dot(hb, wg_ref[:, br * D_MODEL:(br + 1) * D_MODEL],
                                preferred_element_type=F32))
        merged = gate * proj if merged is None else merged + gate * proj
    mix = jnp.dot(merged.astype(BF16), wo_ref[...], preferred_element_type=F32)
    out_ref[...] = x + mod_ref[M_GATE_M:M_GATE_M + 1, :] * mix


def _merge(x, mod_l, vec_l, oa, ob, oc, wg, wa, wb, wc, wo, latent):
    t = x.shape[0]
    per = DEC_SEQ // TM_MERGE
    mod_idx = (lambda i: (1 + i // per, 0, 0)) if latent else (lambda i: (0, 0, 0))
    const2 = lambda i: (0, 0)
    o_spec = pl.BlockSpec((4, TM_MERGE, LANES), lambda i: (0, i, 0))
    width = 4 * LANES
    return pl.pallas_call(
        _merge_kernel,
        out_shape=jax.ShapeDtypeStruct((t, D_MODEL), F32),
        grid=(t // TM_MERGE,),
        in_specs=[
            pl.BlockSpec((TM_MERGE, D_MODEL), lambda i: (i, 0)),
            pl.BlockSpec((None, 6, D_MODEL), mod_idx),
            pl.BlockSpec((N_VEC_ROWS, D_MODEL), const2),
            o_spec, o_spec, o_spec,
            pl.BlockSpec((D_MODEL, 3 * D_MODEL), const2),
            pl.BlockSpec((width, D_MODEL), const2),
            pl.BlockSpec((width, D_MODEL), const2),
            pl.BlockSpec((width, D_MODEL), const2),
            pl.BlockSpec((D_MODEL, D_MODEL), const2),
        ],
        out_specs=pl.BlockSpec((TM_MERGE, D_MODEL), lambda i: (i, 0)),
        compiler_params=_cparams(("arbitrary",)),
        name="merge_latent" if latent else "merge_context",
    )(x, mod_l, vec_l, oa, ob, oc, wg, wa, wb, wc, wo)


def _zero_rows(a, zero_rows):
    if not zero_rows:
        return a
    sub = lax.broadcasted_iota(jnp.int32, (8, a.shape[1]), 0)
    pieces, cur = [], 0
    for r in sorted(zero_rows):
        base = (r // 8) * 8
        if base > cur:
            pieces.append(a[cur:base])
        pieces.append(jnp.where(sub == (r % 8), 0.0, a[base:base + 8]))
        cur = base + 8
    if cur < a.shape[0]:
        pieces.append(a[cur:])
    return jnp.concatenate(pieces, axis=0)


def _ffn_kernel(seq_len, x_ref, xp_ref, xn_ref, mod_ref, vec_ref, wu_ref, cw_ref, cb_ref, wd_ref,
                out_ref, h_scr, u_scr, act_scr):
    rows = x_ref.shape[0]
    ext = rows + 2 * FFN_HALO
    gain = vec_ref[V_NORM_FFN:V_NORM_FFN + 1, :]
    scale = 1.0 + mod_ref[M_SCALE_F:M_SCALE_F + 1, :]
    shift = mod_ref[M_SHIFT_F:M_SHIFT_F + 1, :]

    def hmod(x):
        return _rms(x, gain) * scale + shift

    row0 = pl.program_id(0) * rows
    keep_prev = (row0 & (seq_len - 1)) != 0
    keep_next = ((row0 + rows) & (seq_len - 1)) != 0
    h_scr[0:FFN_HALO, :] = jnp.where(keep_prev, hmod(xp_ref[...]), 0.0).astype(BF16)
    h_scr[FFN_HALO:FFN_HALO + rows, :] = hmod(x_ref[...]).astype(BF16)
    h_scr[FFN_HALO + rows:ext, :] = jnp.where(keep_next, hmod(xn_ref[...]), 0.0).astype(BF16)
    starts = list(range(seq_len, rows, seq_len))
    ends = [r - 1 for r in starts]

    def up(slot, k):
        u_scr[slot, 0] = jnp.dot(h_scr[...], wu_ref[k], preferred_element_type=F32)
        u_scr[slot, 1] = jnp.dot(h_scr[...], wu_ref[N_FF_CHUNKS + k], preferred_element_type=F32)

    def conv(slot, k):
        pad = 8
        for r0 in range(0, rows, FFN_CONV_ROWS):
            blk_starts = [r - r0 for r in starts if r0 <= r < r0 + FFN_CONV_ROWS]
            blk_ends = [r - r0 for r in ends if r0 <= r < r0 + FFN_CONV_ROWS]

            def branch(b, j):
                lo_row = FFN_HALO + r0 - pad
                u = u_scr[slot, b, lo_row:lo_row + FFN_CONV_ROWS + 2 * pad, :]
                n = FFN_CONV_ROWS + 2 * pad
                prev = _zero_rows(pltpu.roll(u, 1, 0)[pad:pad + FFN_CONV_ROWS], blk_starts)
                nxt = _zero_rows(pltpu.roll(u, n - 1, 0)[pad:pad + FFN_CONV_ROWS], blk_ends)
                cur = u[pad:pad + FFN_CONV_ROWS]
                cw = cw_ref[j]
                return prev * cw[0:1, :] + cur * cw[1:2, :] + nxt * cw[2:3, :] + cb_ref[j]

            a = branch(0, k)
            g = branch(1, N_FF_CHUNKS + k)
            act_scr[slot, r0:r0 + FFN_CONV_ROWS, :] = (a * (g * _sigmoid(g))).astype(BF16)

    def down(slot, k):
        for n0 in range(0, D_MODEL, MXU_N):
            out_ref[:, n0:n0 + MXU_N] += jnp.dot(act_scr[slot], wd_ref[k, :, n0:n0 + MXU_N],
                                                 preferred_element_type=F32)

    def stage(k, k_mod):
        up(k_mod % FFN_SLOTS, k)
        conv((k_mod - 1) % FFN_SLOTS, k - 1)
        down((k_mod - 2) % FFN_SLOTS, k - 2)

    out_ref[...] = jnp.zeros_like(out_ref)
    up(0, 0)
    up(1, 1)
    conv(0, 0)

    def body(j, carry):
        k = 2 + FFN_SLOTS * j
        for i in range(FFN_SLOTS):
            stage(k + i, 2 + i)
        return carry

    last = N_FF_CHUNKS - 1
    n_trips = (last - 1) // FFN_SLOTS
    assert 2 + n_trips * FFN_SLOTS == last + 1
    lax.fori_loop(0, n_trips, body, 0)
    conv(last % FFN_SLOTS, last)
    down((last - 1) % FFN_SLOTS, last - 1)
    down(last % FFN_SLOTS, last)
    out_ref[...] = x_ref[...] + mod_ref[M_GATE_F:M_GATE_F + 1, :] * out_ref[...]


def _ffn(x, mod_l, vec_l, wu, cw, cb, wd, latent):
    t = x.shape[0]
    per = DEC_SEQ // TM_FFN
    mod_idx = (lambda i: (1 + i // per, 0, 0)) if latent else (lambda i: (0, 0, 0))
    seq_len = DEC_SEQ if latent else SEQ
    halo_per_tile = TM_FFN // FFN_HALO
    n_halo = t // FFN_HALO
    const3 = lambda i: (0, 0, 0)
    resident = pl.Buffered(1)
    return pl.pallas_call(
        functools.partial(_ffn_kernel, seq_len),
        out_shape=jax.ShapeDtypeStruct((t, D_MODEL), F32),
        grid=(t // TM_FFN,),
        in_specs=[
            pl.BlockSpec((TM_FFN, D_MODEL), lambda i: (i, 0)),
            pl.BlockSpec((FFN_HALO, D_MODEL), lambda i: (jnp.maximum(i * halo_per_tile - 1, 0), 0)),
            pl.BlockSpec((FFN_HALO, D_MODEL),
                         lambda i: (jnp.minimum((i + 1) * halo_per_tile, n_halo - 1), 0)),
            pl.BlockSpec((None, 6, D_MODEL), mod_idx),
            pl.BlockSpec((N_VEC_ROWS, D_MODEL), lambda i: (0, 0)),
            pl.BlockSpec((2 * N_FF_CHUNKS, D_MODEL, TF), const3, pipeline_mode=resident),
            pl.BlockSpec((2 * N_FF_CHUNKS, 3, TF), const3),
            pl.BlockSpec((2 * N_FF_CHUNKS, 1, TF), const3),
            pl.BlockSpec((N_FF_CHUNKS, TF, D_MODEL), const3, pipeline_mode=resident),
        ],
        out_specs=pl.BlockSpec((TM_FFN, D_MODEL), lambda i: (i, 0)),
        scratch_shapes=[pltpu.VMEM((TM_FFN + 2 * FFN_HALO, D_MODEL), BF16),
                        pltpu.VMEM((FFN_SLOTS, 2, TM_FFN + 2 * FFN_HALO, TF), F32),
                        pltpu.VMEM((FFN_SLOTS, TM_FFN, TF), BF16)],
        compiler_params=_cparams(("arbitrary",)),
        name="ffn_latent" if latent else "ffn_context",
    )(x, x, x, mod_l, vec_l, wu, cw, cb, wd)


def _rope_angles(n_tokens, dim):
    rows = n_tokens // GRID_W
    row = jnp.repeat(jnp.arange(rows, dtype=F32), GRID_W)
    col = jnp.tile(jnp.arange(GRID_W, dtype=F32), rows)
    quarter = dim // 4
    inv = 1.0 / (ROPE_BASE ** (jnp.arange(quarter, dtype=F32) / quarter))
    ang = jnp.concatenate([row[:, None] * inv, col[:, None] * inv], axis=-1)
    return jnp.cos(ang), jnp.sin(ang)


def _rope_tables():
    t = DEC_SEQ
    cos_h, sin_h = _rope_angles(t, HEAD_DIM)
    c64 = jnp.tile(cos_h, (1, 4))
    s64 = jnp.concatenate([-sin_h, -sin_h, sin_h, sin_h], axis=1)
    cos_r, sin_r = _rope_angles(t, MLA_ROPE_DIM)
    fill = LANES // 2 - MLA_ROPE_DIM // 2
    ones, zeros = jnp.ones((t, fill), F32), jnp.zeros((t, fill), F32)
    cm = jnp.concatenate([cos_r, ones, cos_r, ones], axis=1)
    sm = jnp.concatenate([-sin_r, zeros, sin_r, zeros], axis=1)
    return jnp.stack([c64, s64, cm, sm])


def _interleave_pairs(w):
    lead = w.shape[:-1]
    w = w.reshape(lead + (-1, 2, 2, HEAD_DIM // 2))
    return jnp.swapaxes(w, -2, -3).reshape(lead + (-1,))


def _place96(x):
    half = MLA_ROPE_DIM // 2
    cut = LANES // 2 - half
    pad = jnp.zeros(x.shape[:-1] + (LANES - MLA_QK_DIM,), x.dtype)
    return jnp.concatenate(
        [x[..., MLA_NOPE_DIM:MLA_NOPE_DIM + half], x[..., :cut], x[..., MLA_NOPE_DIM + half:],
         x[..., cut:MLA_NOPE_DIM], pad], axis=-1)


def _place_nope(x):
    rope = jnp.zeros(x.shape[:-1] + (MLA_ROPE_DIM,), x.dtype)
    return _place96(jnp.concatenate([x, rope], axis=-1))


def _place_rope(x):
    nope = jnp.zeros(x.shape[:-1] + (MLA_NOPE_DIM,), x.dtype)
    return _place96(jnp.concatenate([nope, x], axis=-1))


def _unplace_rope(x):
    half = MLA_ROPE_DIM // 2
    return jnp.concatenate([x[..., :half], x[..., LANES // 2:LANES // 2 + half]], axis=-1)


def _heads96(v):
    return jnp.tile(_place96(v), MLA_HEADS)


def _vec_pack(l, lam_init, interleaved, norm_mix, norm_ffn, diff_qk_norm, mla_q_norm, mla_kv_norm,
              mla_qk_norm, gqa_qk_norm, diff_out_norm):
    pair = _interleave_pairs if interleaved else (lambda v: v)

    def row(v):
        return jnp.pad(v.astype(F32), (0, D_MODEL - v.shape[0]))
    rows = [None] * N_VEC_ROWS
    rows[V_NORM_MIX] = row(norm_mix[l])
    rows[V_NORM_FFN] = row(norm_ffn[l])
    rows[V_AQ] = row(pair(jnp.tile(diff_qk_norm[l, 0], 8)) * HEAD_DIM ** -0.5 * LOG2E)
    rows[V_AK] = row(pair(jnp.tile(diff_qk_norm[l, 1], 8)))
    rows[V_BQD] = row(mla_q_norm[l])
    rows[V_BQ] = row(_heads96(mla_qk_norm[l, 0]) * MLA_QK_DIM ** -0.5 * LOG2E)
    rows[V_CKV] = row(mla_kv_norm[l])
    rows[V_BK] = row(_heads96(mla_qk_norm[l, 1]))
    rows[V_CQ] = row(pair(jnp.tile(gqa_qk_norm[l, 0], 8)) * HEAD_DIM ** -0.5 * LOG2E)
    rows[V_CK] = row(pair(jnp.tile(gqa_qk_norm[l, 1], 4)))
    rows[V_AOUT] = row(diff_out_norm[l] * (1.0 - lam_init))
    zero = jnp.zeros((D_MODEL,), F32)
    return jnp.stack([r if r is not None else zero for r in rows])


def _dup_heads64(w, n_heads):
    lead = w.shape[:-1]
    w = w.reshape(lead + (n_heads, 1, HEAD_DIM))
    return jnp.broadcast_to(w, lead + (n_heads, 2, HEAD_DIM)).reshape(lead + (n_heads * LANES,))


def _w1_layout(w_in_l, interleaved):
    pair = _interleave_pairs if interleaved else (lambda w: w)
    aq, ak, av = w_in_l[:, 0:512], w_in_l[:, 512:1024], w_in_l[:, 1024:1536]
    bqd = w_in_l[:, 1536:1792]
    ckv = w_in_l[:, 1792:1920]
    kpe = w_in_l[:, 1920:1952]
    cq = w_in_l[:, 1952:2464]
    ck = w_in_l[:, 2464:2592]
    cv = w_in_l[:, 2592:2720]
    w1 = jnp.concatenate(
        [pair(aq), pair(ak), av, bqd, ckv, _place_rope(kpe), pair(cq),
         pair(_dup_heads64(ck, GQA_KV_HEADS)), _dup_heads64(cv, GQA_KV_HEADS)], axis=1)
    return w1.astype(BF16)


def _wq_layout(w_q_up_l):
    w = _place96(w_q_up_l.reshape(MLA_Q_RANK, MLA_HEADS, MLA_QK_DIM))
    return w.reshape(MLA_Q_RANK, MLA_HEADS * LANES).astype(BF16)


def _wkv_layout(w_kv_up_l):
    w = w_kv_up_l.reshape(MLA_KV_RANK, MLA_HEADS, MLA_NOPE_DIM + MLA_V_DIM)
    k = _place_nope(w[:, :, :MLA_NOPE_DIM])
    v = w[:, :, MLA_NOPE_DIM:]
    return jnp.concatenate(
        [k.reshape(MLA_KV_RANK, MLA_HEADS * LANES), v.reshape(MLA_KV_RANK, MLA_HEADS * MLA_V_DIM)],
        axis=1).astype(BF16)


def _groups(x, n_groups):
    b, l, s, _ = x.shape
    return jnp.transpose(x.reshape(b, l, s, n_groups, LANES), (1, 0, 3, 2, 4)).astype(BF16)


def kernel(x_prompt, x_sample, c, cache_diff_k, cache_diff_v, cache_mla_ckv, cache_mla_kpe,
           cache_gqa_k, cache_gqa_v, c_ctx, w_ada, b_ada, norm_mix, w_in, diff_qk_norm,
           diff_lambda, diff_out_norm, mla_q_norm, w_mla_q_up, mla_kv_norm, w_mla_kv_up,
           mla_qk_norm, gqa_qk_norm, w_branch_a, w_branch_b, w_branch_c, w_o, norm_ffn,
           w_up, conv_w, conv_b, w_down):
    lam_inits = [0.8 - 0.6 * math.exp(-0.3 * l) for l in range(DEPTH)]
    vecs, vecs_lat = (jnp.stack([
        _vec_pack(l, lam_inits[l], interleaved, norm_mix, norm_ffn, diff_qk_norm, mla_q_norm,
                  mla_kv_norm, mla_qk_norm, gqa_qk_norm, diff_out_norm) for l in range(DEPTH)])
        for interleaved in (False, True))
    wkv_all = jnp.stack([_wkv_layout(w_mla_kv_up[l]) for l in range(DEPTH)])

    cvec = jnp.concatenate(
        [c_ctx[None, :], c, jnp.zeros((N_MOD_ROWS - 1 - DEC_BATCH, D_MODEL), F32)], axis=0)
    mods = _mods(cvec, w_ada, b_ada).reshape(DEPTH, N_MOD_ROWS, 6, D_MODEL)

    rope_tab = _rope_tables()

    ctx_bk, ctx_bv = _ctx_mla(cache_mla_ckv, _place_rope(cache_mla_kpe), wkv_all, vecs)
    nb, nl, ns = DEC_BATCH, DEPTH, PAST_LEN
    ctx = {
        "ak": _groups(_interleave_pairs(cache_diff_k.reshape(nb, nl, ns, 512)), 4),
        "av": _groups(cache_diff_v.reshape(nb, nl, ns, 512), 4),
        "bk": ctx_bk,
        "bv": ctx_bv,
        "ck": _groups(_interleave_pairs(_dup_heads64(cache_gqa_k.reshape(nb, nl, ns, 128), GQA_KV_HEADS)), 2),
        "cv": _groups(_dup_heads64(cache_gqa_v.reshape(nb, nl, ns, 128), GQA_KV_HEADS), 2),
    }

    y_p = x_prompt.reshape(BATCH * SEQ, D_MODEL)
    y_s = x_sample.reshape(DEC_BATCH * DEC_SEQ, D_MODEL)
    states = []
    for l in range(DEPTH):
        w1, w1_lat = _w1_layout(w_in[l], False), _w1_layout(w_in[l], True)
        wg = w_in[l][:, IN_GATES_COL:].astype(BF16)
        wq = _wq_layout(w_mla_q_up[l])
        wkv = wkv_all[l]
        wa, wb, wc = (w.astype(BF16) for w in (w_branch_a[l], w_branch_b[l], w_branch_c[l]))
        wo = w_o[l].astype(BF16)
        nch = 2 * N_FF_CHUNKS
        wu = jnp.transpose(w_up[l].astype(BF16).reshape(D_MODEL, nch, TF), (1, 0, 2))
        wd = w_down[l].astype(BF16).reshape(N_FF_CHUNKS, TF, D_MODEL)
        cw = jnp.transpose(conv_w[l].reshape(3, nch, TF), (1, 0, 2))
        cb = conv_b[l].reshape(nch, 1, TF)
        vec_l, mod_l, lam_l = vecs[l], mods[l], diff_lambda[l]

        qkv, st_ak, st_av, st_ckv, st_kpe, st_ck, st_cv = _inproj(
            y_p, mod_l, vec_l, w1, wq, wkv, None, latent=False)
        states.append((st_ak, st_av, st_ckv, _unplace_rope(st_kpe), st_ck, st_cv))
        oa, ob, oc = _attention(qkv, l, lam_inits[l], lam_l, vec_l, None, latent=False)
        y_p = _merge(y_p, mod_l, vec_l, oa, ob, oc, wg, wa, wb, wc, wo, latent=False)
        y_p = _ffn(y_p, mod_l, vec_l, wu, cw, cb, wd, latent=False)

        qkv = _inproj(y_s, mod_l, vecs_lat[l], w1_lat, wq, wkv, rope_tab, latent=True)
        oa, ob, oc = _attention(qkv, l, lam_inits[l], lam_l, vec_l, ctx, latent=True)
        y_s = _merge(y_s, mod_l, vec_l, oa, ob, oc, wg, wa, wb, wc, wo, latent=True)
        y_s = _ffn(y_s, mod_l, vec_l, wu, cw, cb, wd, latent=True)

    def stacked(i, shape):
        return jnp.stack([s[i].reshape((BATCH, SEQ) + shape) for s in states], axis=1)

    return (
        y_p.reshape(BATCH, SEQ, D_MODEL),
        y_s.reshape(DEC_BATCH, DEC_SEQ, D_MODEL),
        stacked(0, (2, DIFF_HEADS, HEAD_DIM)),
        stacked(1, (DIFF_HEADS, DIFF_V_DIM)),
        stacked(2, (MLA_KV_RANK,)),
        stacked(3, (MLA_ROPE_DIM,)),
        stacked(4, (GQA_KV_HEADS, HEAD_DIM)),
        stacked(5, (GQA_KV_HEADS, HEAD_DIM)),
    )
```

```python
import functools
import math

import jax
import jax.numpy as jnp
import numpy as np
from jax import lax
from jax.experimental import pallas as pl
from jax.experimental.pallas import tpu as pltpu

D_MODEL = 1024
BATCH = 32
SEQ = 256
DEPTH = 4
DEC_BATCH = 4
DEC_SEQ = 2048
PAST_LEN = 256
GRID_W = 64
HEAD_DIM = 64
DIFF_HEADS = 4
DIFF_V_DIM = 2 * HEAD_DIM
MLA_HEADS = 8
MLA_Q_RANK = 256
MLA_KV_RANK = 128
MLA_NOPE_DIM = 64
MLA_ROPE_DIM = 32
MLA_QK_DIM = MLA_NOPE_DIM + MLA_ROPE_DIM
MLA_V_DIM = 64
GQA_HEADS = 8
GQA_KV_HEADS = 2
D_FF = 2816
ROPE_BASE = 10000.0
NORM_EPS = 1e-6
LOG2E = math.log2(math.e)

LANES = 128
MXU_N = 256
VMEM_LIMIT_BYTES = 56 * 1024 * 1024

F32 = jnp.float32
BF16 = jnp.bfloat16

C_AQ, C_AK, C_AV = 0, 512, 1024
C_BQD, C_CKV, C_KPE = 1536, 1792, 1920
C_CQ, C_CK, C_CV = 2048, 2560, 2816
W1_COLS = 3072
IN_GATES_COL = 2720

G_BQ, G_BK, G_AQ, G_AK, G_AV, G_BV, G_CQ, G_CK, G_CV = 0, 8, 16, 20, 24, 28, 32, 36, 38
N_GROUPS = 40

V_NORM_MIX, V_NORM_FFN, V_AQ, V_AK, V_BQD, V_BQ, V_CKV, V_BK, V_CQ, V_CK, V_AOUT = range(11)
N_VEC_ROWS = 16

M_SHIFT_M, M_SCALE_M, M_GATE_M, M_SHIFT_F, M_SCALE_F, M_GATE_F = range(6)
N_MOD_ROWS = 8

TM_IN = 256
TQ = 256
CTX_SEQS_PER_STEP = 2
TM_MERGE = 512
TM_FFN = 1024
FFN_CONV_ROWS = 128
FFN_HALO = 16
TF = 256
N_FF_CHUNKS = D_FF // TF
FFN_SLOTS = 3
TN_ADA = 1536


def _cparams(sem, flags=None):
    return pltpu.CompilerParams(dimension_semantics=sem, vmem_limit_bytes=VMEM_LIMIT_BYTES, flags=flags)


def _rms(x, gain):
    ms = jnp.mean(x * x, axis=-1, keepdims=True)
    return x * lax.rsqrt(ms + NORM_EPS) * gain


def _lane_lo(rows):
    return lax.broadcasted_iota(jnp.int32, (rows, LANES), 1) < HEAD_DIM


def _first_head_lanes(rows, interleaved):
    lane = lax.broadcasted_iota(jnp.int32, (rows, LANES), 1)
    if interleaved:
        return (lane & (HEAD_DIM // 2)) == 0
    return lane < HEAD_DIM


def _same_head_matrix(interleaved):
    r = lax.broadcasted_iota(jnp.int32, (LANES, LANES), 0)
    c = lax.broadcasted_iota(jnp.int32, (LANES, LANES), 1)
    if interleaved:
        same = (r & (HEAD_DIM // 2)) == (c & (HEAD_DIM // 2))
    else:
        same = (r < HEAD_DIM) == (c < HEAD_DIM)
    return jnp.where(same, 1.0, 0.0).astype(BF16)


def _head_norms_lane_sum(groups, gains, width):
    outs = []
    for g, gain in zip(groups, gains):
        ssq = jnp.sum(g * g, axis=-1, keepdims=True)
        outs.append(g * lax.rsqrt(ssq * (1.0 / width) + NORM_EPS) * gain)
    return outs


def _pair_norms_lane_sum(groups, gains, lo):
    outs = []
    for g, gain in zip(groups, gains):
        sq = g * g
        s_all = jnp.sum(sq, axis=-1, keepdims=True)
        s_lo = jnp.sum(jnp.where(lo, sq, 0.0), axis=-1, keepdims=True)
        ssq = jnp.where(lo, s_lo, s_all - s_lo)
        outs.append(g * lax.rsqrt(ssq * (1.0 / HEAD_DIM) + NORM_EPS) * gain)
    return outs


def _head_norms(groups, gains, ones_mat, width):
    rows = groups[0].shape[0]
    sq = jnp.concatenate([(g * g).astype(BF16) for g in groups], axis=0)
    ssq = jnp.dot(sq, ones_mat, preferred_element_type=F32)
    outs = []
    for i, (g, gain) in enumerate(zip(groups, gains)):
        s = ssq[i * rows:(i + 1) * rows]
        outs.append(g * lax.rsqrt(s * (1.0 / width) + NORM_EPS) * gain)
    return outs


def _rope(xg, c, s):
    return xg * c + pltpu.roll(xg, LANES // 2, 1) * s


def _sigmoid(z):
    return 1.0 / (1.0 + jnp.exp(-z))


def _half_heads(qg, lo):
    qf = qg.astype(F32)
    return jnp.where(lo, qf, 0.0).astype(BF16), jnp.where(lo, 0.0, qf).astype(BF16)


def _mods_kernel(c_ref, w_ref, b_ref, o_ref):
    c = c_ref[...]
    a = c * _sigmoid(c)
    w = w_ref[...].astype(BF16)
    a_hi = a.astype(BF16)
    a_lo = (a - a_hi.astype(F32)).astype(BF16)
    acc = jnp.dot(a_hi, w, preferred_element_type=F32) + jnp.dot(a_lo, w, preferred_element_type=F32)
    o_ref[...] = acc + b_ref[...]


def _mods(cvec, w_ada, b_ada):
    n = 6 * D_MODEL
    return pl.pallas_call(
        _mods_kernel,
        out_shape=jax.ShapeDtypeStruct((DEPTH, N_MOD_ROWS, n), F32),
        grid=(DEPTH, n // TN_ADA),
        in_specs=[
            pl.BlockSpec((N_MOD_ROWS, D_MODEL), lambda l, j: (0, 0)),
            pl.BlockSpec((None, D_MODEL, TN_ADA), lambda l, j: (l, 0, j)),
            pl.BlockSpec((None, 1, TN_ADA), lambda l, j: (l, 0, j)),
        ],
        out_specs=pl.BlockSpec((None, N_MOD_ROWS, TN_ADA), lambda l, j: (l, 0, j)),
        compiler_params=_cparams(("arbitrary", "arbitrary")),
        name="ada_mods",
    )(cvec, w_ada, b_ada.reshape(DEPTH, 1, n))


def _mla_head_norms(groups, gains, on_mxu):
    if on_mxu:
        return _head_norms(groups, gains, jnp.ones((LANES, LANES), BF16), MLA_QK_DIM)
    return _head_norms_lane_sum(groups, gains, MLA_QK_DIM)


def _mla_expand(ckv, kpe_grp, wkv_ref, vec_ref, on_mxu=False):
    ckvn = _rms(ckv, vec_ref[V_CKV:V_CKV + 1, 0:MLA_KV_RANK]).astype(BF16)
    kv = jnp.dot(ckvn, wkv_ref[...], preferred_element_type=F32)
    kpre = [kv[:, h * LANES:(h + 1) * LANES] + kpe_grp for h in range(MLA_HEADS)]
    gains = [vec_ref[V_BK:V_BK + 1, h * LANES:(h + 1) * LANES] for h in range(MLA_HEADS)]
    return _mla_head_norms(kpre, gains, on_mxu), kv[:, MLA_HEADS * LANES:]


def _ctx_mla_kernel(ckv_ref, kpe_ref, wkv_ref, vec_ref, k_ref, v_ref):
    keys, vals = _mla_expand(ckv_ref[...], kpe_ref[...], wkv_ref, vec_ref)
    for h in range(MLA_HEADS):
        k_ref[h] = keys[h].astype(BF16)
    for p in range(MLA_HEADS // 2):
        v_ref[p] = vals[:, p * LANES:(p + 1) * LANES].astype(BF16)


def _ctx_mla(cache_ckv, cache_kpe_grp, wkv, vecs):
    return pl.pallas_call(
        _ctx_mla_kernel,
        out_shape=(
            jax.ShapeDtypeStruct((DEPTH, DEC_BATCH, MLA_HEADS, PAST_LEN, LANES), BF16),
            jax.ShapeDtypeStruct((DEPTH, DEC_BATCH, MLA_HEADS // 2, PAST_LEN, LANES), BF16),
        ),
        grid=(DEPTH, DEC_BATCH),
        in_specs=[
            pl.BlockSpec((None, None, PAST_LEN, MLA_KV_RANK), lambda l, b: (b, l, 0, 0)),
            pl.BlockSpec((None, None, PAST_LEN, LANES), lambda l, b: (b, l, 0, 0)),
            pl.BlockSpec((None, MLA_KV_RANK, 12 * LANES), lambda l, b: (l, 0, 0)),
            pl.BlockSpec((None, N_VEC_ROWS, D_MODEL), lambda l, b: (l, 0, 0)),
        ],
        out_specs=(
            pl.BlockSpec((None, None, MLA_HEADS, PAST_LEN, LANES), lambda l, b: (l, b, 0, 0, 0)),
            pl.BlockSpec((None, None, MLA_HEADS // 2, PAST_LEN, LANES), lambda l, b: (l, b, 0, 0, 0)),
        ),
        compiler_params=_cparams(("arbitrary", "arbitrary")),
        name="ctx_mla_expand",
    )(cache_ckv, cache_kpe_grp, wkv, vecs)


def _inproj_kernel(latent, x_ref, mod_ref, vec_ref, w1_ref, wq_ref, wkv_ref, *rest):
    if latent:
        rope_ref, qkv_ref, p_scr = rest
    else:
        qkv_ref, st_ak, st_av, st_ckv, st_kpe, st_ck, st_cv, p_scr = rest
    rows = x_ref.shape[0]
    lo = _first_head_lanes(rows, interleaved=latent)

    x = x_ref[...]
    h = _rms(x, vec_ref[V_NORM_MIX:V_NORM_MIX + 1, :])
    h = h * (1.0 + mod_ref[M_SCALE_M:M_SCALE_M + 1, :]) + mod_ref[M_SHIFT_M:M_SHIFT_M + 1, :]
    p_scr[...] = jnp.dot(h.astype(BF16), w1_ref[...], preferred_element_type=F32)

    def grp(col, g):
        return p_scr[:, col + g * LANES:col + (g + 1) * LANES]

    def vec(row, g):
        return vec_ref[row:row + 1, g * LANES:(g + 1) * LANES]

    def rope64(y):
        if not latent:
            return y
        return _rope(y, rope_ref[0], rope_ref[1])

    def rope_tail(y):
        if not latent:
            return y
        return _rope(y, rope_ref[2], rope_ref[3])

    sections = ((C_AQ, V_AQ, 4), (C_AK, V_AK, 4), (C_CQ, V_CQ, 4), (C_CK, V_CK, 2))
    raw = [grp(col, g) for col, _, n in sections for g in range(n)]
    gains = [vec(row, g) for _, row, n in sections for g in range(n)]
    if latent:
        normed = _head_norms(raw, gains, _same_head_matrix(interleaved=True), HEAD_DIM)
    else:
        normed = _pair_norms_lane_sum(raw, gains, lo)
    aqn, akn, cqn, ckn = normed[0:4], normed[4:8], normed[8:12], normed[12:14]
    for g in range(4):
        qkv_ref[G_AQ + g] = rope64(aqn[g]).astype(BF16)
        qkv_ref[G_AK + g] = rope64(akn[g]).astype(BF16)
        qkv_ref[G_CQ + g] = rope64(cqn[g]).astype(BF16)
        av = grp(C_AV, g)
        qkv_ref[G_AV + g] = av.astype(BF16)
        if not latent:
            st_ak[:, g * LANES:(g + 1) * LANES] = akn[g]
            st_av[:, g * LANES:(g + 1) * LANES] = av
    cvs = [grp(C_CV, g) for g in range(2)]
    for g in range(2):
        qkv_ref[G_CK + g] = rope64(ckn[g]).astype(BF16)
        qkv_ref[G_CV + g] = cvs[g].astype(BF16)
    if not latent:
        st_ck[...] = jnp.where(lo, ckn[0], ckn[1])
        st_cv[...] = jnp.where(lo, cvs[0], cvs[1])

    bqn = _rms(p_scr[:, C_BQD:C_BQD + MLA_Q_RANK], vec_ref[V_BQD:V_BQD + 1, 0:MLA_Q_RANK]).astype(BF16)
    bq = jnp.dot(bqn, wq_ref[...], preferred_element_type=F32)
    qn = _mla_head_norms([bq[:, hd * LANES:(hd + 1) * LANES] for hd in range(MLA_HEADS)],
                         [vec(V_BQ, hd) for hd in range(MLA_HEADS)], on_mxu=latent)
    ckv = p_scr[:, C_CKV:C_CKV + MLA_KV_RANK]
    kpe_grp = p_scr[:, C_KPE:C_KPE + LANES]
    keys, vals = _mla_expand(ckv, kpe_grp, wkv_ref, vec_ref, on_mxu=latent)
    for hd in range(MLA_HEADS):
        qkv_ref[G_BQ + hd] = rope_tail(qn[hd]).astype(BF16)
        qkv_ref[G_BK + hd] = rope_tail(keys[hd]).astype(BF16)
    for p in range(MLA_HEADS // 2):
        qkv_ref[G_BV + p] = vals[:, p * LANES:(p + 1) * LANES].astype(BF16)
    if not latent:
        st_ckv[...] = ckv
        st_kpe[...] = kpe_grp


def _inproj(x, mod_l, vec_l, w1, wq, wkv, rope_tab, latent):
    t = x.shape[0]
    n_tiles = t // TM_IN
    tiles_per_seq = DEC_SEQ // TM_IN
    if latent:
        mod_idx = lambda i: (1 + i // tiles_per_seq, 0, 0)
    else:
        mod_idx = lambda i: (0, 0, 0)
    const2 = lambda i: (0, 0)
    in_specs = [
        pl.BlockSpec((TM_IN, D_MODEL), lambda i: (i, 0)),
        pl.BlockSpec((None, 6, D_MODEL), mod_idx),
        pl.BlockSpec((N_VEC_ROWS, D_MODEL), const2),
        pl.BlockSpec((D_MODEL, W1_COLS), const2),
        pl.BlockSpec((MLA_Q_RANK, MLA_HEADS * LANES), const2),
        pl.BlockSpec((MLA_KV_RANK, 12 * LANES), const2),
    ]
    args = [x, mod_l, vec_l, w1, wq, wkv]
    qkv_shape = jax.ShapeDtypeStruct((N_GROUPS, t, LANES), BF16)
    qkv_spec = pl.BlockSpec((N_GROUPS, TM_IN, LANES), lambda i: (0, i, 0))
    if latent:
        in_specs.append(pl.BlockSpec((4, TM_IN, LANES), lambda i: (0, i % tiles_per_seq, 0)))
        args.append(rope_tab)
        out_shape = qkv_shape
        out_specs = qkv_spec
    else:
        widths = (512, 512, LANES, LANES, LANES, LANES)
        out_shape = (qkv_shape,) + tuple(jax.ShapeDtypeStruct((t, w), F32) for w in widths)
        out_specs = (qkv_spec,) + tuple(pl.BlockSpec((TM_IN, w), lambda i: (i, 0)) for w in widths)
    return pl.pallas_call(
        functools.partial(_inproj_kernel, latent),
        out_shape=out_shape,
        grid=(n_tiles,),
        in_specs=in_specs,
        out_specs=out_specs,
        scratch_shapes=[pltpu.VMEM((TM_IN, W1_COLS), F32)],
        compiler_params=_cparams(("arbitrary",)),
        name="inproj_latent" if latent else "inproj_context",
    )(*args)


def _attend(q, k_parts, v_parts):
    nt = (((1,), (1,)), ((), ()))
    ss = [lax.dot_general(q, k, nt, preferred_element_type=F32) for k in k_parts]
    m = functools.reduce(jnp.maximum, [jnp.max(s, axis=-1, keepdims=True) for s in ss])
    ps = [jnp.exp2(s - m) for s in ss]
    l = functools.reduce(jnp.add, [jnp.sum(p, axis=-1, keepdims=True) for p in ps])
    o = functools.reduce(
        jnp.add,
        [jnp.dot(p.astype(BF16), v, preferred_element_type=F32) for p, v in zip(ps, v_parts)])
    return o / l


def _seq_rows(ref, n_seq, sq):
    n = ref.shape[1] // n_seq
    return pl.ds(sq * n, n)


def _mla_attn_kernel(has_ctx, n_seq, q_ref, k_ref, v_ref, *rest):
    if has_ctx:
        kc_ref, vc_ref, o_ref = rest
    else:
        (o_ref,) = rest
    lo_out = _lane_lo(q_ref.shape[1] // n_seq)
    for sq in range(n_seq):
        rq, rk = _seq_rows(q_ref, n_seq, sq), _seq_rows(k_ref, n_seq, sq)
        for u in range(MLA_HEADS // 2):
            outs = []
            for half in range(2):
                hd = 2 * u + half
                ks = ([kc_ref[hd]] if has_ctx else []) + [k_ref[hd, rk, :]]
                vs = ([vc_ref[u]] if has_ctx else []) + [v_ref[u, rk, :]]
                outs.append(_attend(q_ref[hd, rq, :], ks, vs))
            o_ref[u, rq, :] = jnp.where(lo_out, outs[0], outs[1]).astype(BF16)


def _gqa_attn_kernel(has_ctx, n_seq, q_ref, k_ref, v_ref, *rest):
    if has_ctx:
        kc_ref, vc_ref, o_ref = rest
    else:
        (o_ref,) = rest
    tq = q_ref.shape[1] // n_seq
    lo_out = _lane_lo(tq)
    lo_q = _first_head_lanes(tq, interleaved=has_ctx)
    for sq in range(n_seq):
        rq, rk = _seq_rows(q_ref, n_seq, sq), _seq_rows(k_ref, n_seq, sq)
        for u in range(GQA_HEADS // 2):
            kv = u // 2
            ks = ([kc_ref[kv]] if has_ctx else []) + [k_ref[kv, rk, :]]
            vs = ([vc_ref[kv]] if has_ctx else []) + [v_ref[kv, rk, :]]
            q_a, q_b = _half_heads(q_ref[u, rq, :], lo_q)
            o_ref[u, rq, :] = jnp.where(lo_out, _attend(q_a, ks, vs), _attend(q_b, ks, vs)).astype(BF16)


def _diff_attn_kernel(has_ctx, n_seq, lam_init, q_ref, k_ref, v_ref, lam_ref, vec_ref, *rest):
    if has_ctx:
        kc_ref, vc_ref, o_ref = rest
    else:
        (o_ref,) = rest
    lo_q = _first_head_lanes(q_ref.shape[1] // n_seq, interleaved=has_ctx)
    lv = lam_ref[...]
    lam = (jnp.exp(jnp.sum(lv[0:1] * lv[1:2], axis=-1, keepdims=True))
           - jnp.exp(jnp.sum(lv[2:3] * lv[3:4], axis=-1, keepdims=True)) + lam_init)
    gain = vec_ref[V_AOUT:V_AOUT + 1, 0:LANES]
    for sq in range(n_seq):
        rq, rk = _seq_rows(q_ref, n_seq, sq), _seq_rows(k_ref, n_seq, sq)
        for hd in range(DIFF_HEADS):
            pp, half = hd // 2, hd % 2
            vs = ([vc_ref[hd]] if has_ctx else []) + [v_ref[hd, rk, :]]
            os = []
            for m in range(2):
                g = 2 * m + pp
                q = _half_heads(q_ref[g, rq, :], lo_q)[half]
                ks = ([kc_ref[g]] if has_ctx else []) + [k_ref[g, rk, :]]
                os.append(_attend(q, ks, vs))
            o_ref[hd, rq, :] = _rms(os[0] - lam * os[1], gain).astype(BF16)


def _attn_specs(latent, groups_q, blk_q, groups_k, blk_k, groups_v, blk_v):
    if latent:
        per = DEC_SEQ // TQ
        q_idx = lambda b, j: (blk_q, b * per + j, 0)
        k_idx = lambda b, j: (blk_k, b, 0)
        v_idx = lambda b, j: (blk_v, b, 0)
        rows_q, rows_k = TQ, DEC_SEQ
    else:
        q_idx = lambda i: (blk_q, i, 0)
        k_idx = lambda i: (blk_k, i, 0)
        v_idx = lambda i: (blk_v, i, 0)
        rows_q = rows_k = CTX_SEQS_PER_STEP * SEQ
    return [
        pl.BlockSpec((groups_q, rows_q, LANES), q_idx),
        pl.BlockSpec((groups_k, rows_k, LANES), k_idx),
        pl.BlockSpec((groups_v, rows_k, LANES), v_idx),
    ]


def _ctx_spec(groups, layer):
    return pl.BlockSpec((None, None, groups, PAST_LEN, LANES), lambda b, j: (layer, b, 0, 0, 0))


def _attn_call(kernel, name, latent, specs, args, t):
    if latent:
        grid = (DEC_BATCH, DEC_SEQ // TQ)
        o_idx = lambda b, j: (0, b * (DEC_SEQ // TQ) + j, 0)
        sem = ("arbitrary", "arbitrary")
        rows = TQ
    else:
        rows = CTX_SEQS_PER_STEP * SEQ
        grid = (t // rows,)
        o_idx = lambda i: (0, i, 0)
        sem = ("arbitrary",)
    return pl.pallas_call(
        kernel,
        out_shape=jax.ShapeDtypeStruct((4, t, LANES), BF16),
        grid=grid,
        in_specs=specs,
        out_specs=pl.BlockSpec((4, rows, LANES), o_idx),
        compiler_params=_cparams(sem),
        name=name + ("_latent" if latent else "_context"),
    )(*args)


def _attention(qkv, layer, lam_init, lam_l, vec_l, ctx, latent):
    t = qkv.shape[1]
    n_seq = 1 if latent else CTX_SEQS_PER_STEP
    if latent:
        const2 = lambda b, j: (0, 0)
    else:
        const2 = lambda i: (0, 0)

    specs = _attn_specs(latent, 4, G_AQ // 4, 4, G_AK // 4, 4, G_AV // 4)
    specs += [pl.BlockSpec((4, HEAD_DIM), const2), pl.BlockSpec((N_VEC_ROWS, D_MODEL), const2)]
    args = [qkv, qkv, qkv, lam_l, vec_l]
    if latent:
        specs += [_ctx_spec(4, layer), _ctx_spec(4, layer)]
        args += [ctx["ak"], ctx["av"]]
    oa = _attn_call(functools.partial(_diff_attn_kernel, latent, n_seq, lam_init), "diff_attn", latent,
                    specs, args, t)

    specs = _attn_specs(latent, 8, G_BQ // 8, 8, G_BK // 8, 4, G_BV // 4)
    args = [qkv, qkv, qkv]
    if latent:
        specs += [_ctx_spec(8, layer), _ctx_spec(4, layer)]
        args += [ctx["bk"], ctx["bv"]]
    ob = _attn_call(functools.partial(_mla_attn_kernel, latent, n_seq), "mla_attn", latent, specs, args, t)

    specs = _attn_specs(latent, 4, G_CQ // 4, 2, G_CK // 2, 2, G_CV // 2)
    args = [qkv, qkv, qkv]
    if latent:
        specs += [_ctx_spec(2, layer), _ctx_spec(2, layer)]
        args += [ctx["ck"], ctx["cv"]]
    oc = _attn_call(functools.partial(_gqa_attn_kernel, latent, n_seq), "gqa_attn", latent, specs, args, t)
    return oa, ob, oc


def _merge_kernel(x_ref, mod_ref, vec_ref, oa_ref, ob_ref, oc_ref, wg_ref, wa_ref, wb_ref, wc_ref,
                  wo_ref, out_ref):
    x = x_ref[...]
    h = _rms(x, vec_ref[V_NORM_MIX:V_NORM_MIX + 1, :])
    h = h * (1.0 + mod_ref[M_SCALE_M:M_SCALE_M + 1, :]) + mod_ref[M_SHIFT_M:M_SHIFT_M + 1, :]
    hb = h.astype(BF16)
    merged = None
    for br, (o_ref, w_ref) in enumerate(((oa_ref, wa_ref), (ob_ref, wb_ref), (oc_ref, wc_ref))):
        o = jnp.concatenate([o_ref[g] for g in range(4)], axis=1)
        proj = jnp.dot(o, w_ref[...], preferred_element_type=F32)
        gate = _sigmoid(jnp.dot(hb, wg_ref[:, br * D_MODEL:(br + 1) * D_MODEL],
                                preferred_element_type=F32))
        merged = gate * proj if merged is None else merged + gate * proj
    mix = jnp.dot(merged.astype(BF16), wo_ref[...], preferred_element_type=F32)
    out_ref[...] = x + mod_ref[M_GATE_M:M_GATE_M + 1, :] * mix


def _merge(x, mod_l, vec_l, oa, ob, oc, wg, wa, wb, wc, wo, latent):
    t = x.shape[0]
    per = DEC_SEQ // TM_MERGE
    mod_idx = (lambda i: (1 + i // per, 0, 0)) if latent else (lambda i: (0, 0, 0))
    const2 = lambda i: (0, 0)
    o_spec = pl.BlockSpec((4, TM_MERGE, LANES), lambda i: (0, i, 0))
    width = 4 * LANES
    return pl.pallas_call(
        _merge_kernel,
        out_shape=jax.ShapeDtypeStruct((t, D_MODEL), F32),
        grid=(t // TM_MERGE,),
        in_specs=[
            pl.BlockSpec((TM_MERGE, D_MODEL), lambda i: (i, 0)),
            pl.BlockSpec((None, 6, D_MODEL), mod_idx),
            pl.BlockSpec((N_VEC_ROWS, D_MODEL), const2),
            o_spec, o_spec, o_spec,
            pl.BlockSpec((D_MODEL, 3 * D_MODEL), const2),
            pl.BlockSpec((width, D_MODEL), const2),
            pl.BlockSpec((width, D_MODEL), const2),
            pl.BlockSpec((width, D_MODEL), const2),
            pl.BlockSpec((D_MODEL, D_MODEL), const2),
        ],
        out_specs=pl.BlockSpec((TM_MERGE, D_MODEL), lambda i: (i, 0)),
        compiler_params=_cparams(("arbitrary",)),
        name="merge_latent" if latent else "merge_context",
    )(x, mod_l, vec_l, oa, ob, oc, wg, wa, wb, wc, wo)


def _zero_rows(a, zero_rows):
    if not zero_rows:
        return a
    sub = lax.broadcasted_iota(jnp.int32, (8, a.shape[1]), 0)
    pieces, cur = [], 0
    for r in sorted(zero_rows):
        base = (r // 8) * 8
        if base > cur:
            pieces.append(a[cur:base])
        pieces.append(jnp.where(sub == (r % 8), 0.0, a[base:base + 8]))
        cur = base + 8
    if cur < a.shape[0]:
        pieces.append(a[cur:])
    return jnp.concatenate(pieces, axis=0)


def _ffn_kernel(seq_len, x_ref, xp_ref, xn_ref, mod_ref, vec_ref, wu_ref, cw_ref, cb_ref, wd_ref,
                out_ref, h_scr, u_scr, act_scr):
    rows = x_ref.shape[0]
    ext = rows + 2 * FFN_HALO
    gain = vec_ref[V_NORM_FFN:V_NORM_FFN + 1, :]
    scale = 1.0 + mod_ref[M_SCALE_F:M_SCALE_F + 1, :]
    shift = mod_ref[M_SHIFT_F:M_SHIFT_F + 1, :]

    def hmod(x):
        return _rms(x, gain) * scale + shift

    row0 = pl.program_id(0) * rows
    keep_prev = (row0 & (seq_len - 1)) != 0
    keep_next = ((row0 + rows) & (seq_len - 1)) != 0
    h_scr[0:FFN_HALO, :] = jnp.where(keep_prev, hmod(xp_ref[...]), 0.0).astype(BF16)
    h_scr[FFN_HALO:FFN_HALO + rows, :] = hmod(x_ref[...]).astype(BF16)
    h_scr[FFN_HALO + rows:ext, :] = jnp.where(keep_next, hmod(xn_ref[...]), 0.0).astype(BF16)
    starts = list(range(seq_len, rows, seq_len))
    ends = [r - 1 for r in starts]

    def up(slot, k):
        u_scr[slot, 0] = jnp.dot(h_scr[...], wu_ref[k], preferred_element_type=F32)
        u_scr[slot, 1] = jnp.dot(h_scr[...], wu_ref[N_FF_CHUNKS + k], preferred_element_type=F32)

    def conv(slot, k):
        pad = 8
        for r0 in range(0, rows, FFN_CONV_ROWS):
            blk_starts = [r - r0 for r in starts if r0 <= r < r0 + FFN_CONV_ROWS]
            blk_ends = [r - r0 for r in ends if r0 <= r < r0 + FFN_CONV_ROWS]

            def branch(b, j):
                lo_row = FFN_HALO + r0 - pad
                u = u_scr[slot, b, lo_row:lo_row + FFN_CONV_ROWS + 2 * pad, :]
                n = FFN_CONV_ROWS + 2 * pad
                prev = _zero_rows(pltpu.roll(u, 1, 0)[pad:pad + FFN_CONV_ROWS], blk_starts)
                nxt = _zero_rows(pltpu.roll(u, n - 1, 0)[pad:pad + FFN_CONV_ROWS], blk_ends)
                cur = u[pad:pad + FFN_CONV_ROWS]
                cw = cw_ref[j]
                return prev * cw[0:1, :] + cur * cw[1:2, :] + nxt * cw[2:3, :] + cb_ref[j]

            a = branch(0, k)
            g = branch(1, N_FF_CHUNKS + k)
            act_scr[slot, r0:r0 + FFN_CONV_ROWS, :] = (a * (g * _sigmoid(g))).astype(BF16)

    def down(slot, k):
        for n0 in range(0, D_MODEL, MXU_N):
            out_ref[:, n0:n0 + MXU_N] += jnp.dot(act_scr[slot], wd_ref[k, :, n0:n0 + MXU_N],
                                                 preferred_element_type=F32)

    def stage(k, k_mod):
        up(k_mod % FFN_SLOTS, k)
        conv((k_mod - 1) % FFN_SLOTS, k - 1)
        down((k_mod - 2) % FFN_SLOTS, k - 2)

    out_ref[...] = jnp.zeros_like(out_ref)
    up(0, 0)
    up(1, 1)
    conv(0, 0)

    def body(j, carry):
        k = 2 + FFN_SLOTS * j
        for i in range(FFN_SLOTS):
            stage(k + i, 2 + i)
        return carry

    last = N_FF_CHUNKS - 1
    n_trips = (last - 1) // FFN_SLOTS
    assert 2 + n_trips * FFN_SLOTS == last + 1
    lax.fori_loop(0, n_trips, body, 0)
    conv(last % FFN_SLOTS, last)
    down((last - 1) % FFN_SLOTS, last - 1)
    down(last % FFN_SLOTS, last)
    out_ref[...] = x_ref[...] + mod_ref[M_GATE_F:M_GATE_F + 1, :] * out_ref[...]


def _ffn(x, mod_l, vec_l, wu, cw, cb, wd, latent):
    t = x.shape[0]
    per = DEC_SEQ // TM_FFN
    mod_idx = (lambda i: (1 + i // per, 0, 0)) if latent else (lambda i: (0, 0, 0))
    seq_len = DEC_SEQ if latent else SEQ
    halo_per_tile = TM_FFN // FFN_HALO
    n_halo = t // FFN_HALO
    const3 = lambda i: (0, 0, 0)
    resident = pl.Buffered(1)
    return pl.pallas_call(
        functools.partial(_ffn_kernel, seq_len),
        out_shape=jax.ShapeDtypeStruct((t, D_MODEL), F32),
        grid=(t // TM_FFN,),
        in_specs=[
            pl.BlockSpec((TM_FFN, D_MODEL), lambda i: (i, 0)),
            pl.BlockSpec((FFN_HALO, D_MODEL), lambda i: (jnp.maximum(i * halo_per_tile - 1, 0), 0)),
            pl.BlockSpec((FFN_HALO, D_MODEL),
                         lambda i: (jnp.minimum((i + 1) * halo_per_tile, n_halo - 1), 0)),
            pl.BlockSpec((None, 6, D_MODEL), mod_idx),
            pl.BlockSpec((N_VEC_ROWS, D_MODEL), lambda i: (0, 0)),
            pl.BlockSpec((2 * N_FF_CHUNKS, D_MODEL, TF), const3, pipeline_mode=resident),
            pl.BlockSpec((2 * N_FF_CHUNKS, 3, TF), const3),
            pl.BlockSpec((2 * N_FF_CHUNKS, 1, TF), const3),
            pl.BlockSpec((N_FF_CHUNKS, TF, D_MODEL), const3, pipeline_mode=resident),
        ],
        out_specs=pl.BlockSpec((TM_FFN, D_MODEL), lambda i: (i, 0)),
        scratch_shapes=[pltpu.VMEM((TM_FFN + 2 * FFN_HALO, D_MODEL), BF16),
                        pltpu.VMEM((FFN_SLOTS, 2, TM_FFN + 2 * FFN_HALO, TF), F32),
                        pltpu.VMEM((FFN_SLOTS, TM_FFN, TF), BF16)],
        compiler_params=_cparams(("arbitrary",)),
        name="ffn_latent" if latent else "ffn_context",
    )(x, x, x, mod_l, vec_l, wu, cw, cb, wd)


def _rope_angles(n_tokens, dim):
    rows = n_tokens // GRID_W
    row = jnp.repeat(jnp.arange(rows, dtype=F32), GRID_W)
    col = jnp.tile(jnp.arange(GRID_W, dtype=F32), rows)
    quarter = dim // 4
    inv = 1.0 / (ROPE_BASE ** (jnp.arange(quarter, dtype=F32) / quarter))
    ang = jnp.concatenate([row[:, None] * inv, col[:, None] * inv], axis=-1)
    return jnp.cos(ang), jnp.sin(ang)


def _rope_tables():
    t = DEC_SEQ
    cos_h, sin_h = _rope_angles(t, HEAD_DIM)
    c64 = jnp.tile(cos_h, (1, 4))
    s64 = jnp.concatenate([-sin_h, -sin_h, sin_h, sin_h], axis=1)
    cos_r, sin_r = _rope_angles(t, MLA_ROPE_DIM)
    fill = LANES // 2 - MLA_ROPE_DIM // 2
    ones, zeros = jnp.ones((t, fill), F32), jnp.zeros((t, fill), F32)
    cm = jnp.concatenate([cos_r, ones, cos_r, ones], axis=1)
    sm = jnp.concatenate([-sin_r, zeros, sin_r, zeros], axis=1)
    return jnp.stack([c64, s64, cm, sm])


def _interleave_pairs(w):
    lead = w.shape[:-1]
    w = w.reshape(lead + (-1, 2, 2, HEAD_DIM // 2))
    return jnp.swapaxes(w, -2, -3).reshape(lead + (-1,))


def _place96(x):
    half = MLA_ROPE_DIM // 2
    cut = LANES // 2 - half
    pad = jnp.zeros(x.shape[:-1] + (LANES - MLA_QK_DIM,), x.dtype)
    return jnp.concatenate(
        [x[..., MLA_NOPE_DIM:MLA_NOPE_DIM + half], x[..., :cut], x[..., MLA_NOPE_DIM + half:],
         x[..., cut:MLA_NOPE_DIM], pad], axis=-1)


def _place_nope(x):
    rope = jnp.zeros(x.shape[:-1] + (MLA_ROPE_DIM,), x.dtype)
    return _place96(jnp.concatenate([x, rope], axis=-1))


def _place_rope(x):
    nope = jnp.zeros(x.shape[:-1] + (MLA_NOPE_DIM,), x.dtype)
    return _place96(jnp.concatenate([nope, x], axis=-1))


def _unplace_rope(x):
    half = MLA_ROPE_DIM // 2
    return jnp.concatenate([x[..., :half], x[..., LANES // 2:LANES // 2 + half]], axis=-1)


def _heads96(v):
    return jnp.tile(_place96(v), MLA_HEADS)


def _vec_pack(l, lam_init, interleaved, norm_mix, norm_ffn, diff_qk_norm, mla_q_norm, mla_kv_norm,
              mla_qk_norm, gqa_qk_norm, diff_out_norm):
    pair = _interleave_pairs if interleaved else (lambda v: v)

    def row(v):
        return jnp.pad(v.astype(F32), (0, D_MODEL - v.shape[0]))
    rows = [None] * N_VEC_ROWS
    rows[V_NORM_MIX] = row(norm_mix[l])
    rows[V_NORM_FFN] = row(norm_ffn[l])
    rows[V_AQ] = row(pair(jnp.tile(diff_qk_norm[l, 0], 8)) * HEAD_DIM ** -0.5 * LOG2E)
    rows[V_AK] = row(pair(jnp.tile(diff_qk_norm[l, 1], 8)))
    rows[V_BQD] = row(mla_q_norm[l])
    rows[V_BQ] = row(_heads96(mla_qk_norm[l, 0]) * MLA_QK_DIM ** -0.5 * LOG2E)
    rows[V_CKV] = row(mla_kv_norm[l])
    rows[V_BK] = row(_heads96(mla_qk_norm[l, 1]))
    rows[V_CQ] = row(pair(jnp.tile(gqa_qk_norm[l, 0], 8)) * HEAD_DIM ** -0.5 * LOG2E)
    rows[V_CK] = row(pair(jnp.tile(gqa_qk_norm[l, 1], 4)))
    rows[V_AOUT] = row(diff_out_norm[l] * (1.0 - lam_init))
    zero = jnp.zeros((D_MODEL,), F32)
    return jnp.stack([r if r is not None else zero for r in rows])


def _dup_heads64(w, n_heads):
    lead = w.shape[:-1]
    w = w.reshape(lead + (n_heads, 1, HEAD_DIM))
    return jnp.broadcast_to(w, lead + (n_heads, 2, HEAD_DIM)).reshape(lead + (n_heads * LANES,))


def _w1_layout(w_in_l, interleaved):
    pair = _interleave_pairs if interleaved else (lambda w: w)
    aq, ak, av = w_in_l[:, 0:512], w_in_l[:, 512:1024], w_in_l[:, 1024:1536]
    bqd = w_in_l[:, 1536:1792]
    ckv = w_in_l[:, 1792:1920]
    kpe = w_in_l[:, 1920:1952]
    cq = w_in_l[:, 1952:2464]
    ck = w_in_l[:, 2464:2592]
    cv = w_in_l[:, 2592:2720]
    w1 = jnp.concatenate(
        [pair(aq), pair(ak), av, bqd, ckv, _place_rope(kpe), pair(cq),
         pair(_dup_heads64(ck, GQA_KV_HEADS)), _dup_heads64(cv, GQA_KV_HEADS)], axis=1)
    return w1.astype(BF16)


def _wq_layout(w_q_up_l):
    w = _place96(w_q_up_l.reshape(MLA_Q_RANK, MLA_HEADS, MLA_QK_DIM))
    return w.reshape(MLA_Q_RANK, MLA_HEADS * LANES).astype(BF16)


def _wkv_layout(w_kv_up_l):
    w = w_kv_up_l.reshape(MLA_KV_RANK, MLA_HEADS, MLA_NOPE_DIM + MLA_V_DIM)
    k = _place_nope(w[:, :, :MLA_NOPE_DIM])
    v = w[:, :, MLA_NOPE_DIM:]
    return jnp.concatenate(
        [k.reshape(MLA_KV_RANK, MLA_HEADS * LANES), v.reshape(MLA_KV_RANK, MLA_HEADS * MLA_V_DIM)],
        axis=1).astype(BF16)


def _groups(x, n_groups):
    b, l, s, _ = x.shape
    return jnp.transpose(x.reshape(b, l, s, n_groups, LANES), (1, 0, 3, 2, 4)).astype(BF16)


def kernel(x_prompt, x_sample, c, cache_diff_k, cache_diff_v, cache_mla_ckv, cache_mla_kpe,
           cache_gqa_k, cache_gqa_v, c_ctx, w_ada, b_ada, norm_mix, w_in, diff_qk_norm,
           diff_lambda, diff_out_norm, mla_q_norm, w_mla_q_up, mla_kv_norm, w_mla_kv_up,
           mla_qk_norm, gqa_qk_norm, w_branch_a, w_branch_b, w_branch_c, w_o, norm_ffn,
           w_up, conv_w, conv_b, w_down):
    lam_inits = [0.8 - 0.6 * math.exp(-0.3 * l) for l in range(DEPTH)]
    vecs, vecs_lat = (jnp.stack([
        _vec_pack(l, lam_inits[l], interleaved, norm_mix, norm_ffn, diff_qk_norm, mla_q_norm,
                  mla_kv_norm, mla_qk_norm, gqa_qk_norm, diff_out_norm) for l in range(DEPTH)])
        for interleaved in (False, True))
    wkv_all = jnp.stack([_wkv_layout(w_mla_kv_up[l]) for l in range(DEPTH)])

    cvec = jnp.concatenate(
        [c_ctx[None, :], c, jnp.zeros((N_MOD_ROWS - 1 - DEC_BATCH, D_MODEL), F32)], axis=0)
    mods = _mods(cvec, w_ada, b_ada).reshape(DEPTH, N_MOD_ROWS, 6, D_MODEL)

    rope_tab = _rope_tables()

    ctx_bk, ctx_bv = _ctx_mla(cache_mla_ckv, _place_rope(cache_mla_kpe), wkv_all, vecs)
    nb, nl, ns = DEC_BATCH, DEPTH, PAST_LEN
    ctx = {
        "ak": _groups(_interleave_pairs(cache_diff_k.reshape(nb, nl, ns, 512)), 4),
        "av": _groups(cache_diff_v.reshape(nb, nl, ns, 512), 4),
        "bk": ctx_bk,
        "bv": ctx_bv,
        "ck": _groups(_interleave_pairs(_dup_heads64(cache_gqa_k.reshape(nb, nl, ns, 128), GQA_KV_HEADS)), 2),
        "cv": _groups(_dup_heads64(cache_gqa_v.reshape(nb, nl, ns, 128), GQA_KV_HEADS), 2),
    }

    y_p = x_prompt.reshape(BATCH * SEQ, D_MODEL)
    y_s = x_sample.reshape(DEC_BATCH * DEC_SEQ, D_MODEL)
    states = []
    for l in range(DEPTH):
        w1, w1_lat = _w1_layout(w_in[l], False), _w1_layout(w_in[l], True)
        wg = w_in[l][:, IN_GATES_COL:].astype(BF16)
        wq = _wq_layout(w_mla_q_up[l])
        wkv = wkv_all[l]
        wa, wb, wc = (w.astype(BF16) for w in (w_branch_a[l], w_branch_b[l], w_branch_c[l]))
        wo = w_o[l].astype(BF16)
        nch = 2 * N_FF_CHUNKS
        wu = jnp.transpose(w_up[l].reshape(D_MODEL, nch, TF), (1, 0, 2)).astype(BF16)
        wd = w_down[l].astype(BF16).reshape(N_FF_CHUNKS, TF, D_MODEL)
        cw = jnp.transpose(conv_w[l].reshape(3, nch, TF), (1, 0, 2))
        cb = conv_b[l].reshape(nch, 1, TF)
        vec_l, mod_l, lam_l = vecs[l], mods[l], diff_lambda[l]

        qkv, st_ak, st_av, st_ckv, st_kpe, st_ck, st_cv = _inproj(
            y_p, mod_l, vec_l, w1, wq, wkv, None, latent=False)
        states.append((st_ak, st_av, st_ckv, _unplace_rope(st_kpe), st_ck, st_cv))
        oa, ob, oc = _attention(qkv, l, lam_inits[l], lam_l, vec_l, None, latent=False)
        y_p = _merge(y_p, mod_l, vec_l, oa, ob, oc, wg, wa, wb, wc, wo, latent=False)
        y_p = _ffn(y_p, mod_l, vec_l, wu, cw, cb, wd, latent=False)

        qkv = _inproj(y_s, mod_l, vecs_lat[l], w1_lat, wq, wkv, rope_tab, latent=True)
        oa, ob, oc = _attention(qkv, l, lam_inits[l], lam_l, vec_l, ctx, latent=True)
        y_s = _merge(y_s, mod_l, vec_l, oa, ob, oc, wg, wa, wb, wc, wo, latent=True)
        y_s = _ffn(y_s, mod_l, vec_l, wu, cw, cb, wd, latent=True)

    def stacked(i, shape):
        return jnp.stack([s[i].reshape((BATCH, SEQ) + shape) for s in states], axis=1)

    return (
        y_p.reshape(BATCH, SEQ, D_MODEL),
        y_s.reshape(DEC_BATCH, DEC_SEQ, D_MODEL),
        stacked(0, (2, DIFF_HEADS, HEAD_DIM)),
        stacked(1, (DIFF_HEADS, DIFF_V_DIM)),
        stacked(2, (MLA_KV_RANK,)),
        stacked(3, (MLA_ROPE_DIM,)),
        stacked(4, (GQA_KV_HEADS, HEAD_DIM)),
        stacked(5, (GQA_KV_HEADS, HEAD_DIM)),
    )
```

```python
import functools
import math

import jax
import jax.numpy as jnp
import numpy as np
from jax import lax
from jax.experimental import pallas as pl
from jax.experimental.pallas import tpu as pltpu

D_MODEL = 1024
BATCH = 32
SEQ = 256
DEPTH = 4
DEC_BATCH = 4
DEC_SEQ = 2048
PAST_LEN = 256
GRID_W = 64
HEAD_DIM = 64
DIFF_HEADS = 4
DIFF_V_DIM = 2 * HEAD_DIM
MLA_HEADS = 8
MLA_Q_RANK = 256
MLA_KV_RANK = 128
MLA_NOPE_DIM = 64
MLA_ROPE_DIM = 32
MLA_QK_DIM = MLA_NOPE_DIM + MLA_ROPE_DIM
MLA_V_DIM = 64
GQA_HEADS = 8
GQA_KV_HEADS = 2
D_FF = 2816
ROPE_BASE = 10000.0
NORM_EPS = 1e-6
LOG2E = math.log2(math.e)

LANES = 128
MXU_N = 256
VMEM_LIMIT_BYTES = 56 * 1024 * 1024

F32 = jnp.float32
BF16 = jnp.bfloat16

C_AQ, C_AK, C_AV = 0, 512, 1024
C_BQD, C_CKV, C_KPE = 1536, 1792, 1920
C_CQ, C_CK, C_CV = 2048, 2560, 2816
W1_COLS = 3072
IN_GATES_COL = 2720

G_BQ, G_BK, G_AQ, G_AK, G_AV, G_BV, G_CQ, G_CK, G_CV = 0, 8, 16, 20, 24, 28, 32, 36, 38
N_GROUPS = 40
STATE_WIDTHS = (512, 512, LANES, LANES, LANES, LANES)
N_STATES = len(STATE_WIDTHS)

V_NORM_MIX, V_NORM_FFN, V_AQ, V_AK, V_BQD, V_BQ, V_CKV, V_BK, V_CQ, V_CK, V_AOUT = range(11)
N_VEC_ROWS = 16

M_SHIFT_M, M_SCALE_M, M_GATE_M, M_SHIFT_F, M_SCALE_F, M_GATE_F = range(6)
N_MOD_ROWS = 8

TM_IN = 256
TQ = 512
CTX_SEQS_PER_STEP = 4
TM_MERGE = 512
TM_FFN = 1024
FFN_CONV_ROWS = 128
FFN_HALO = 16
TF = 256
N_FF_CHUNKS = D_FF // TF
FFN_SLOTS = 3
TN_ADA = 1536


def _cparams(sem, flags=None):
    return pltpu.CompilerParams(dimension_semantics=sem, vmem_limit_bytes=VMEM_LIMIT_BYTES, flags=flags)


def _rms(x, gain):
    ms = jnp.mean(x * x, axis=-1, keepdims=True)
    return x * lax.rsqrt(ms + NORM_EPS) * gain


def _lane_lo(rows):
    return lax.broadcasted_iota(jnp.int32, (rows, LANES), 1) < HEAD_DIM


def _first_head_lanes(rows, interleaved):
    lane = lax.broadcasted_iota(jnp.int32, (rows, LANES), 1)
    if interleaved:
        return (lane & (HEAD_DIM // 2)) == 0
    return lane < HEAD_DIM


def _same_head_matrix(interleaved):
    r = lax.broadcasted_iota(jnp.int32, (LANES, LANES), 0)
    c = lax.broadcasted_iota(jnp.int32, (LANES, LANES), 1)
    if interleaved:
        same = (r & (HEAD_DIM // 2)) == (c & (HEAD_DIM // 2))
    else:
        same = (r < HEAD_DIM) == (c < HEAD_DIM)
    return jnp.where(same, 1.0, 0.0).astype(BF16)


def _head_norms_lane_sum(groups, gains, width):
    outs = []
    for g, gain in zip(groups, gains):
        ssq = jnp.sum(g * g, axis=-1, keepdims=True)
        outs.append(g * lax.rsqrt(ssq * (1.0 / width) + NORM_EPS) * gain)
    return outs


def _pair_norms_lane_sum(groups, gains, lo):
    outs = []
    for g, gain in zip(groups, gains):
        sq = g * g
        s_all = jnp.sum(sq, axis=-1, keepdims=True)
        s_lo = jnp.sum(jnp.where(lo, sq, 0.0), axis=-1, keepdims=True)
        ssq = jnp.where(lo, s_lo, s_all - s_lo)
        outs.append(g * lax.rsqrt(ssq * (1.0 / HEAD_DIM) + NORM_EPS) * gain)
    return outs


def _head_norms(groups, gains, ones_mat, width):
    rows = groups[0].shape[0]
    sq = jnp.concatenate([(g * g).astype(BF16) for g in groups], axis=0)
    ssq = jnp.dot(sq, ones_mat, preferred_element_type=F32)
    outs = []
    for i, (g, gain) in enumerate(zip(groups, gains)):
        s = ssq[i * rows:(i + 1) * rows]
        outs.append(g * lax.rsqrt(s * (1.0 / width) + NORM_EPS) * gain)
    return outs


def _rope(xg, c, s):
    return xg * c + pltpu.roll(xg, LANES // 2, 1) * s


def _sigmoid(z):
    return 1.0 / (1.0 + jnp.exp(-z))


def _half_heads(qg, lo):
    qf = qg.astype(F32)
    return jnp.where(lo, qf, 0.0).astype(BF16), jnp.where(lo, 0.0, qf).astype(BF16)


def _mods_kernel(c_ref, w_ref, b_ref, o_ref):
    c = c_ref[...]
    a = c * _sigmoid(c)
    w = w_ref[...].astype(BF16)
    a_hi = a.astype(BF16)
    a_lo = (a - a_hi.astype(F32)).astype(BF16)
    acc = jnp.dot(a_hi, w, preferred_element_type=F32) + jnp.dot(a_lo, w, preferred_element_type=F32)
    o_ref[...] = acc + b_ref[...]


def _mods(cvec, w_ada, b_ada):
    n = 6 * D_MODEL
    return pl.pallas_call(
        _mods_kernel,
        out_shape=jax.ShapeDtypeStruct((DEPTH, N_MOD_ROWS, n), F32),
        grid=(DEPTH, n // TN_ADA),
        in_specs=[
            pl.BlockSpec((N_MOD_ROWS, D_MODEL), lambda l, j: (0, 0)),
            pl.BlockSpec((None, D_MODEL, TN_ADA), lambda l, j: (l, 0, j)),
            pl.BlockSpec((None, 1, TN_ADA), lambda l, j: (l, 0, j)),
        ],
        out_specs=pl.BlockSpec((None, N_MOD_ROWS, TN_ADA), lambda l, j: (l, 0, j)),
        compiler_params=_cparams(("arbitrary", "arbitrary")),
        name="ada_mods",
    )(cvec, w_ada, b_ada.reshape(DEPTH, 1, n))


def _mla_head_norms(groups, gains, on_mxu):
    if on_mxu:
        return _head_norms(groups, gains, jnp.ones((LANES, LANES), BF16), MLA_QK_DIM)
    return _head_norms_lane_sum(groups, gains, MLA_QK_DIM)


def _mla_expand(ckv, kpe_grp, wkv_ref, vec_ref, on_mxu=False):
    ckvn = _rms(ckv, vec_ref[V_CKV:V_CKV + 1, 0:MLA_KV_RANK]).astype(BF16)
    kv = jnp.dot(ckvn, wkv_ref[...], preferred_element_type=F32)
    kpre = [kv[:, h * LANES:(h + 1) * LANES] + kpe_grp for h in range(MLA_HEADS)]
    gains = [vec_ref[V_BK:V_BK + 1, h * LANES:(h + 1) * LANES] for h in range(MLA_HEADS)]
    return _mla_head_norms(kpre, gains, on_mxu), kv[:, MLA_HEADS * LANES:]


def _ctx_mla_kernel(ckv_ref, kpe_ref, wkv_ref, vec_ref, k_ref, v_ref):
    keys, vals = _mla_expand(ckv_ref[...], kpe_ref[...], wkv_ref, vec_ref)
    for h in range(MLA_HEADS):
        k_ref[h] = keys[h].astype(BF16)
    for p in range(MLA_HEADS // 2):
        v_ref[p] = vals[:, p * LANES:(p + 1) * LANES].astype(BF16)


def _ctx_mla(cache_ckv, cache_kpe_grp, wkv, vecs):
    return pl.pallas_call(
        _ctx_mla_kernel,
        out_shape=(
            jax.ShapeDtypeStruct((DEPTH, DEC_BATCH, MLA_HEADS, PAST_LEN, LANES), BF16),
            jax.ShapeDtypeStruct((DEPTH, DEC_BATCH, MLA_HEADS // 2, PAST_LEN, LANES), BF16),
        ),
        grid=(DEPTH, DEC_BATCH),
        in_specs=[
            pl.BlockSpec((None, None, PAST_LEN, MLA_KV_RANK), lambda l, b: (b, l, 0, 0)),
            pl.BlockSpec((None, None, PAST_LEN, LANES), lambda l, b: (b, l, 0, 0)),
            pl.BlockSpec((None, MLA_KV_RANK, 12 * LANES), lambda l, b: (l, 0, 0)),
            pl.BlockSpec((None, N_VEC_ROWS, D_MODEL), lambda l, b: (l, 0, 0)),
        ],
        out_specs=(
            pl.BlockSpec((None, None, MLA_HEADS, PAST_LEN, LANES), lambda l, b: (l, b, 0, 0, 0)),
            pl.BlockSpec((None, None, MLA_HEADS // 2, PAST_LEN, LANES), lambda l, b: (l, b, 0, 0, 0)),
        ),
        compiler_params=_cparams(("arbitrary", "arbitrary")),
        name="ctx_mla_expand",
    )(cache_ckv, cache_kpe_grp, wkv, vecs)


def _inproj_kernel(latent, x_ref, mod_ref, vec_ref, w1_ref, wq_ref, wkv_ref, *rest):
    if latent:
        rope_ref, qkv_ref, p_scr = rest
    else:
        qkv_ref, st_ak, st_av, st_ckv, st_kpe, st_ck, st_cv, p_scr = rest[N_STATES:]
    rows = x_ref.shape[0]
    lo = _first_head_lanes(rows, interleaved=latent)

    x = x_ref[...]
    h = _rms(x, vec_ref[V_NORM_MIX:V_NORM_MIX + 1, :])
    h = h * (1.0 + mod_ref[M_SCALE_M:M_SCALE_M + 1, :]) + mod_ref[M_SHIFT_M:M_SHIFT_M + 1, :]
    p_scr[...] = jnp.dot(h.astype(BF16), w1_ref[...], preferred_element_type=F32)

    def grp(col, g):
        return p_scr[:, col + g * LANES:col + (g + 1) * LANES]

    def vec(row, g):
        return vec_ref[row:row + 1, g * LANES:(g + 1) * LANES]

    def rope64(y):
        if not latent:
            return y
        return _rope(y, rope_ref[0], rope_ref[1])

    def rope_tail(y):
        if not latent:
            return y
        return _rope(y, rope_ref[2], rope_ref[3])

    sections = ((C_AQ, V_AQ, 4), (C_AK, V_AK, 4), (C_CQ, V_CQ, 4), (C_CK, V_CK, 2))
    raw = [grp(col, g) for col, _, n in sections for g in range(n)]
    gains = [vec(row, g) for _, row, n in sections for g in range(n)]
    if latent:
        normed = _head_norms(raw, gains, _same_head_matrix(interleaved=True), HEAD_DIM)
    else:
        normed = _pair_norms_lane_sum(raw, gains, lo)
    aqn, akn, cqn, ckn = normed[0:4], normed[4:8], normed[8:12], normed[12:14]
    for g in range(4):
        qkv_ref[G_AQ + g] = rope64(aqn[g]).astype(BF16)
        qkv_ref[G_AK + g] = rope64(akn[g]).astype(BF16)
        qkv_ref[G_CQ + g] = rope64(cqn[g]).astype(BF16)
        av = grp(C_AV, g)
        qkv_ref[G_AV + g] = av.astype(BF16)
        if not latent:
            st_ak[:, g * LANES:(g + 1) * LANES] = akn[g]
            st_av[:, g * LANES:(g + 1) * LANES] = av
    cvs = [grp(C_CV, g) for g in range(2)]
    for g in range(2):
        qkv_ref[G_CK + g] = rope64(ckn[g]).astype(BF16)
        qkv_ref[G_CV + g] = cvs[g].astype(BF16)
    if not latent:
        st_ck[...] = jnp.where(lo, ckn[0], ckn[1])
        st_cv[...] = jnp.where(lo, cvs[0], cvs[1])

    bqn = _rms(p_scr[:, C_BQD:C_BQD + MLA_Q_RANK], vec_ref[V_BQD:V_BQD + 1, 0:MLA_Q_RANK]).astype(BF16)
    bq = jnp.dot(bqn, wq_ref[...], preferred_element_type=F32)
    qn = _mla_head_norms([bq[:, hd * LANES:(hd + 1) * LANES] for hd in range(MLA_HEADS)],
                         [vec(V_BQ, hd) for hd in range(MLA_HEADS)], on_mxu=latent)
    ckv = p_scr[:, C_CKV:C_CKV + MLA_KV_RANK]
    kpe_grp = p_scr[:, C_KPE:C_KPE + LANES]
    keys, vals = _mla_expand(ckv, kpe_grp, wkv_ref, vec_ref, on_mxu=latent)
    for hd in range(MLA_HEADS):
        qkv_ref[G_BQ + hd] = rope_tail(qn[hd]).astype(BF16)
        qkv_ref[G_BK + hd] = rope_tail(keys[hd]).astype(BF16)
    for p in range(MLA_HEADS // 2):
        qkv_ref[G_BV + p] = vals[:, p * LANES:(p + 1) * LANES].astype(BF16)
    if not latent:
        st_ckv[...] = ckv
        st_kpe[...] = kpe_grp


def _inproj(x, mod_l, vec_l, w1, wq, wkv, rope_tab, latent, layer=None, states=None):
    t = x.shape[0]
    n_tiles = t // TM_IN
    tiles_per_seq = DEC_SEQ // TM_IN
    if latent:
        mod_idx = lambda i: (1 + i // tiles_per_seq, 0, 0)
    else:
        mod_idx = lambda i: (0, 0, 0)
    const2 = lambda i: (0, 0)
    in_specs = [
        pl.BlockSpec((TM_IN, D_MODEL), lambda i: (i, 0)),
        pl.BlockSpec((None, 6, D_MODEL), mod_idx),
        pl.BlockSpec((N_VEC_ROWS, D_MODEL), const2),
        pl.BlockSpec((D_MODEL, W1_COLS), const2),
        pl.BlockSpec((MLA_Q_RANK, MLA_HEADS * LANES), const2),
        pl.BlockSpec((MLA_KV_RANK, 12 * LANES), const2),
    ]
    args = [x, mod_l, vec_l, w1, wq, wkv]
    qkv_shape = jax.ShapeDtypeStruct((N_GROUPS, t, LANES), BF16)
    qkv_spec = pl.BlockSpec((N_GROUPS, TM_IN, LANES), lambda i: (0, i, 0))
    if latent:
        in_specs.append(pl.BlockSpec((4, TM_IN, LANES), lambda i: (0, i % tiles_per_seq, 0)))
        args.append(rope_tab)
        out_shape = qkv_shape
        out_specs = qkv_spec
    else:
        assert TM_IN == SEQ
        aliases = {len(args) + j: 1 + j for j in range(N_STATES)}
        in_specs += [pl.BlockSpec(memory_space=pl.ANY)] * N_STATES
        args += list(states)
        out_shape = (qkv_shape,) + tuple(jax.ShapeDtypeStruct(st.shape, F32) for st in states)
        out_specs = (qkv_spec,) + tuple(
            pl.BlockSpec((None, None, SEQ, st.shape[-1]), lambda i: (i, layer, 0, 0)) for st in states)
    return pl.pallas_call(
        functools.partial(_inproj_kernel, latent),
        out_shape=out_shape,
        grid=(n_tiles,),
        in_specs=in_specs,
        out_specs=out_specs,
        scratch_shapes=[pltpu.VMEM((TM_IN, W1_COLS), F32)],
        input_output_aliases={} if latent else aliases,
        compiler_params=_cparams(("arbitrary",)),
        name="inproj_latent" if latent else "inproj_context",
    )(*args)


def _attend(q, k_parts, v_parts):
    nt = (((1,), (1,)), ((), ()))
    ss = [lax.dot_general(q, k, nt, preferred_element_type=F32) for k in k_parts]
    m = functools.reduce(jnp.maximum, [jnp.max(s, axis=-1, keepdims=True) for s in ss])
    ps = [jnp.exp2(s - m) for s in ss]
    l = functools.reduce(jnp.add, [jnp.sum(p, axis=-1, keepdims=True) for p in ps])
    o = functools.reduce(
        jnp.add,
        [jnp.dot(p.astype(BF16), v, preferred_element_type=F32) for p, v in zip(ps, v_parts)])
    return o / l


def _seq_rows(ref, n_seq, sq):
    n = ref.shape[1] // n_seq
    return pl.ds(sq * n, n)


def _mla_attn_kernel(has_ctx, n_seq, q_ref, k_ref, v_ref, *rest):
    if has_ctx:
        kc_ref, vc_ref, o_ref = rest
    else:
        (o_ref,) = rest
    lo_out = _lane_lo(q_ref.shape[1] // n_seq)
    for sq in range(n_seq):
        rq, rk = _seq_rows(q_ref, n_seq, sq), _seq_rows(k_ref, n_seq, sq)
        for u in range(MLA_HEADS // 2):
            outs = []
            for half in range(2):
                hd = 2 * u + half
                ks = ([kc_ref[hd]] if has_ctx else []) + [k_ref[hd, rk, :]]
                vs = ([vc_ref[u]] if has_ctx else []) + [v_ref[u, rk, :]]
                outs.append(_attend(q_ref[hd, rq, :], ks, vs))
            o_ref[u, rq, :] = jnp.where(lo_out, outs[0], outs[1]).astype(BF16)


def _gqa_attn_kernel(has_ctx, n_seq, q_ref, k_ref, v_ref, *rest):
    if has_ctx:
        kc_ref, vc_ref, o_ref = rest
    else:
        (o_ref,) = rest
    tq = q_ref.shape[1] // n_seq
    lo_out = _lane_lo(tq)
    lo_q = _first_head_lanes(tq, interleaved=has_ctx)
    for sq in range(n_seq):
        rq, rk = _seq_rows(q_ref, n_seq, sq), _seq_rows(k_ref, n_seq, sq)
        for u in range(GQA_HEADS // 2):
            kv = u // 2
            ks = ([kc_ref[kv]] if has_ctx else []) + [k_ref[kv, rk, :]]
            vs = ([vc_ref[kv]] if has_ctx else []) + [v_ref[kv, rk, :]]
            q_a, q_b = _half_heads(q_ref[u, rq, :], lo_q)
            o_ref[u, rq, :] = jnp.where(lo_out, _attend(q_a, ks, vs), _attend(q_b, ks, vs)).astype(BF16)


def _diff_attn_kernel(has_ctx, n_seq, lam_init, q_ref, k_ref, v_ref, lam_ref, vec_ref, *rest):
    if has_ctx:
        kc_ref, vc_ref, o_ref = rest
    else:
        (o_ref,) = rest
    lo_q = _first_head_lanes(q_ref.shape[1] // n_seq, interleaved=has_ctx)
    lv = lam_ref[...]
    lam = (jnp.exp(jnp.sum(lv[0:1] * lv[1:2], axis=-1, keepdims=True))
           - jnp.exp(jnp.sum(lv[2:3] * lv[3:4], axis=-1, keepdims=True)) + lam_init)
    gain = vec_ref[V_AOUT:V_AOUT + 1, 0:LANES]
    for sq in range(n_seq):
        rq, rk = _seq_rows(q_ref, n_seq, sq), _seq_rows(k_ref, n_seq, sq)
        for hd in range(DIFF_HEADS):
            pp, half = hd // 2, hd % 2
            vs = ([vc_ref[hd]] if has_ctx else []) + [v_ref[hd, rk, :]]
            os = []
            for m in range(2):
                g = 2 * m + pp
                q = _half_heads(q_ref[g, rq, :], lo_q)[half]
                ks = ([kc_ref[g]] if has_ctx else []) + [k_ref[g, rk, :]]
                os.append(_attend(q, ks, vs))
            o_ref[hd, rq, :] = _rms(os[0] - lam * os[1], gain).astype(BF16)


def _attn_specs(latent, groups_q, blk_q, groups_k, blk_k, groups_v, blk_v):
    if latent:
        per = DEC_SEQ // TQ
        q_idx = lambda b, j: (blk_q, b * per + j, 0)
        k_idx = lambda b, j: (blk_k, b, 0)
        v_idx = lambda b, j: (blk_v, b, 0)
        rows_q, rows_k = TQ, DEC_SEQ
    else:
        q_idx = lambda i: (blk_q, i, 0)
        k_idx = lambda i: (blk_k, i, 0)
        v_idx = lambda i: (blk_v, i, 0)
        rows_q = rows_k = CTX_SEQS_PER_STEP * SEQ
    return [
        pl.BlockSpec((groups_q, rows_q, LANES), q_idx),
        pl.BlockSpec((groups_k, rows_k, LANES), k_idx),
        pl.BlockSpec((groups_v, rows_k, LANES), v_idx),
    ]


def _ctx_spec(groups, layer):
    return pl.BlockSpec((None, None, groups, PAST_LEN, LANES), lambda b, j: (layer, b, 0, 0, 0))


def _attn_call(kernel, name, latent, specs, args, t):
    if latent:
        grid = (DEC_BATCH, DEC_SEQ // TQ)
        o_idx = lambda b, j: (0, b * (DEC_SEQ // TQ) + j, 0)
        sem = ("arbitrary", "arbitrary")
        rows = TQ
    else:
        rows = CTX_SEQS_PER_STEP * SEQ
        grid = (t // rows,)
        o_idx = lambda i: (0, i, 0)
        sem = ("arbitrary",)
    return pl.pallas_call(
        kernel,
        out_shape=jax.ShapeDtypeStruct((4, t, LANES), BF16),
        grid=grid,
        in_specs=specs,
        out_specs=pl.BlockSpec((4, rows, LANES), o_idx),
        compiler_params=_cparams(sem),
        name=name + ("_latent" if latent else "_context"),
    )(*args)


def _attention(qkv, layer, lam_init, lam_l, vec_l, ctx, latent):
    t = qkv.shape[1]
    n_seq = 1 if latent else CTX_SEQS_PER_STEP
    if latent:
        const2 = lambda b, j: (0, 0)
    else:
        const2 = lambda i: (0, 0)

    specs = _attn_specs(latent, 4, G_AQ // 4, 4, G_AK // 4, 4, G_AV // 4)
    specs += [pl.BlockSpec((4, HEAD_DIM), const2), pl.BlockSpec((N_VEC_ROWS, D_MODEL), const2)]
    args = [qkv, qkv, qkv, lam_l, vec_l]
    if latent:
        specs += [_ctx_spec(4, layer), _ctx_spec(4, layer)]
        args += [ctx["ak"], ctx["av"]]
    oa = _attn_call(functools.partial(_diff_attn_kernel, latent, n_seq, lam_init), "diff_attn", latent,
                    specs, args, t)

    specs = _attn_specs(latent, 8, G_BQ // 8, 8, G_BK // 8, 4, G_BV // 4)
    args = [qkv, qkv, qkv]
    if latent:
        specs += [_ctx_spec(8, layer), _ctx_spec(4, layer)]
        args += [ctx["bk"], ctx["bv"]]
    ob = _attn_call(functools.partial(_mla_attn_kernel, latent, n_seq), "mla_attn", latent, specs, args, t)

    specs = _attn_specs(latent, 4, G_CQ // 4, 2, G_CK // 2, 2, G_CV // 2)
    args = [qkv, qkv, qkv]
    if latent:
        specs += [_ctx_spec(2, layer), _ctx_spec(2, layer)]
        args += [ctx["ck"], ctx["cv"]]
    oc = _attn_call(functools.partial(_gqa_attn_kernel, latent, n_seq), "gqa_attn", latent, specs, args, t)
    return oa, ob, oc


def _merge_kernel(x_ref, mod_ref, vec_ref, oa_ref, ob_ref, oc_ref, wg_ref, wa_ref, wb_ref, wc_ref,
                  wo_ref, out_ref):
    x = x_ref[...]
    h = _rms(x, vec_ref[V_NORM_MIX:V_NORM_MIX + 1, :])
    h = h * (1.0 + mod_ref[M_SCALE_M:M_SCALE_M + 1, :]) + mod_ref[M_SHIFT_M:M_SHIFT_M + 1, :]
    hb = h.astype(BF16)
    merged = None
    for br, (o_ref, w_ref) in enumerate(((oa_ref, wa_ref), (ob_ref, wb_ref), (oc_ref, wc_ref))):
        o = jnp.concatenate([o_ref[g] for g in range(4)], axis=1)
        proj = jnp.dot(o, w_ref[...], preferred_element_type=F32)
        gate = _sigmoid(jnp.dot(hb, wg_ref[:, br * D_MODEL:(br + 1) * D_MODEL],
                                preferred_element_type=F32))
        merged = gate * proj if merged is None else merged + gate * proj
    mix = jnp.dot(merged.astype(BF16), wo_ref[...], preferred_element_type=F32)
    out_ref[...] = x + mod_ref[M_GATE_M:M_GATE_M + 1, :] * mix


def _merge(x, mod_l, vec_l, oa, ob, oc, wg, wa, wb, wc, wo, latent):
    t = x.shape[0]
    per = DEC_SEQ // TM_MERGE
    mod_idx = (lambda i: (1 + i // per, 0, 0)) if latent else (lambda i: (0, 0, 0))
    const2 = lambda i: (0, 0)
    o_spec = pl.BlockSpec((4, TM_MERGE, LANES), lambda i: (0, i, 0))
    width = 4 * LANES
    return pl.pallas_call(
        _merge_kernel,
        out_shape=jax.ShapeDtypeStruct((t, D_MODEL), F32),
        grid=(t // TM_MERGE,),
        in_specs=[
            pl.BlockSpec((TM_MERGE, D_MODEL), lambda i: (i, 0)),
            pl.BlockSpec((None, 6, D_MODEL), mod_idx),
            pl.BlockSpec((N_VEC_ROWS, D_MODEL), const2),
            o_spec, o_spec, o_spec,
            pl.BlockSpec((D_MODEL, 3 * D_MODEL), const2),
            pl.BlockSpec((width, D_MODEL), const2),
            pl.BlockSpec((width, D_MODEL), const2),
            pl.BlockSpec((width, D_MODEL), const2),
            pl.BlockSpec((D_MODEL, D_MODEL), const2),
        ],
        out_specs=pl.BlockSpec((TM_MERGE, D_MODEL), lambda i: (i, 0)),
        compiler_params=_cparams(("arbitrary",)),
        name="merge_latent" if latent else "merge_context",
    )(x, mod_l, vec_l, oa, ob, oc, wg, wa, wb, wc, wo)


def _zero_rows(a, zero_rows):
    if not zero_rows:
        return a
    sub = lax.broadcasted_iota(jnp.int32, (8, a.shape[1]), 0)
    pieces, cur = [], 0
    for r in sorted(zero_rows):
        base = (r // 8) * 8
        if base > cur:
            pieces.append(a[cur:base])
        pieces.append(jnp.where(sub == (r % 8), 0.0, a[base:base + 8]))
        cur = base + 8
    if cur < a.shape[0]:
        pieces.append(a[cur:])
    return jnp.concatenate(pieces, axis=0)


def _ffn_kernel(seq_len, x_ref, xp_ref, xn_ref, mod_ref, vec_ref, wu_ref, cw_ref, cb_ref, wd_ref,
                out_ref, h_scr, u_scr, act_scr):
    rows = x_ref.shape[0]
    ext = rows + 2 * FFN_HALO
    gain = vec_ref[V_NORM_FFN:V_NORM_FFN + 1, :]
    scale = 1.0 + mod_ref[M_SCALE_F:M_SCALE_F + 1, :]
    shift = mod_ref[M_SHIFT_F:M_SHIFT_F + 1, :]

    def hmod(x):
        return _rms(x, gain) * scale + shift

    row0 = pl.program_id(0) * rows
    keep_prev = (row0 & (seq_len - 1)) != 0
    keep_next = ((row0 + rows) & (seq_len - 1)) != 0
    h_scr[0:FFN_HALO, :] = jnp.where(keep_prev, hmod(xp_ref[...]), 0.0).astype(BF16)
    h_scr[FFN_HALO:FFN_HALO + rows, :] = hmod(x_ref[...]).astype(BF16)
    h_scr[FFN_HALO + rows:ext, :] = jnp.where(keep_next, hmod(xn_ref[...]), 0.0).astype(BF16)
    starts = list(range(seq_len, rows, seq_len))
    ends = [r - 1 for r in starts]

    def up(slot, k):
        for b in range(2):
            col = pl.multiple_of((b * N_FF_CHUNKS + k) * TF, TF)
            u_scr[slot, b] = jnp.dot(h_scr[...], wu_ref[:, pl.ds(col, TF)], preferred_element_type=F32)

    def conv(slot, k):
        pad = 8
        for r0 in range(0, rows, FFN_CONV_ROWS):
            blk_starts = [r - r0 for r in starts if r0 <= r < r0 + FFN_CONV_ROWS]
            blk_ends = [r - r0 for r in ends if r0 <= r < r0 + FFN_CONV_ROWS]

            def branch(b, j):
                lo_row = FFN_HALO + r0 - pad
                u = u_scr[slot, b, lo_row:lo_row + FFN_CONV_ROWS + 2 * pad, :]
                n = FFN_CONV_ROWS + 2 * pad
                prev = _zero_rows(pltpu.roll(u, 1, 0)[pad:pad + FFN_CONV_ROWS], blk_starts)
                nxt = _zero_rows(pltpu.roll(u, n - 1, 0)[pad:pad + FFN_CONV_ROWS], blk_ends)
                cur = u[pad:pad + FFN_CONV_ROWS]
                cw = cw_ref[j]
                return prev * cw[0:1, :] + cur * cw[1:2, :] + nxt * cw[2:3, :] + cb_ref[j]

            a = branch(0, k)
            g = branch(1, N_FF_CHUNKS + k)
            act_scr[slot, r0:r0 + FFN_CONV_ROWS, :] = (a * (g * _sigmoid(g))).astype(BF16)

    def down(slot, k):
        for n0 in range(0, D_MODEL, MXU_N):
            out_ref[:, n0:n0 + MXU_N] += jnp.dot(act_scr[slot], wd_ref[k, :, n0:n0 + MXU_N],
                                                 preferred_element_type=F32)

    def stage(k, k_mod):
        up(k_mod % FFN_SLOTS, k)
        conv((k_mod - 1) % FFN_SLOTS, k - 1)
        down((k_mod - 2) % FFN_SLOTS, k - 2)

    out_ref[...] = jnp.zeros_like(out_ref)
    up(0, 0)
    up(1, 1)
    conv(0, 0)

    def body(j, carry):
        k = 2 + FFN_SLOTS * j
        for i in range(FFN_SLOTS):
            stage(k + i, 2 + i)
        return carry

    last = N_FF_CHUNKS - 1
    n_trips = (last - 1) // FFN_SLOTS
    assert 2 + n_trips * FFN_SLOTS == last + 1
    lax.fori_loop(0, n_trips, body, 0)
    conv(last % FFN_SLOTS, last)
    down((last - 1) % FFN_SLOTS, last - 1)
    down(last % FFN_SLOTS, last)
    out_ref[...] = x_ref[...] + mod_ref[M_GATE_F:M_GATE_F + 1, :] * out_ref[...]


def _ffn(x, mod_l, vec_l, wu, cw, cb, wd, latent):
    t = x.shape[0]
    per = DEC_SEQ // TM_FFN
    mod_idx = (lambda i: (1 + i // per, 0, 0)) if latent else (lambda i: (0, 0, 0))
    seq_len = DEC_SEQ if latent else SEQ
    halo_per_tile = TM_FFN // FFN_HALO
    n_halo = t // FFN_HALO
    const3 = lambda i: (0, 0, 0)
    resident = pl.Buffered(1)
    return pl.pallas_call(
        functools.partial(_ffn_kernel, seq_len),
        out_shape=jax.ShapeDtypeStruct((t, D_MODEL), F32),
        grid=(t // TM_FFN,),
        in_specs=[
            pl.BlockSpec((TM_FFN, D_MODEL), lambda i: (i, 0)),
            pl.BlockSpec((FFN_HALO, D_MODEL), lambda i: (jnp.maximum(i * halo_per_tile - 1, 0), 0)),
            pl.BlockSpec((FFN_HALO, D_MODEL),
                         lambda i: (jnp.minimum((i + 1) * halo_per_tile, n_halo - 1), 0)),
            pl.BlockSpec((None, 6, D_MODEL), mod_idx),
            pl.BlockSpec((N_VEC_ROWS, D_MODEL), lambda i: (0, 0)),
            pl.BlockSpec((D_MODEL, 2 * D_FF), lambda i: (0, 0), pipeline_mode=resident),
            pl.BlockSpec((2 * N_FF_CHUNKS, 3, TF), const3),
            pl.BlockSpec((2 * N_FF_CHUNKS, 1, TF), const3),
            pl.BlockSpec((N_FF_CHUNKS, TF, D_MODEL), const3, pipeline_mode=resident),
        ],
        out_specs=pl.BlockSpec((TM_FFN, D_MODEL), lambda i: (i, 0)),
        scratch_shapes=[pltpu.VMEM((TM_FFN + 2 * FFN_HALO, D_MODEL), BF16),
                        pltpu.VMEM((FFN_SLOTS, 2, TM_FFN + 2 * FFN_HALO, TF), F32),
                        pltpu.VMEM((FFN_SLOTS, TM_FFN, TF), BF16)],
        compiler_params=_cparams(("arbitrary",)),
        name="ffn_latent" if latent else "ffn_context",
    )(x, x, x, mod_l, vec_l, wu, cw, cb, wd)


def _rope_angles(n_tokens, dim):
    rows = n_tokens // GRID_W
    row = jnp.repeat(jnp.arange(rows, dtype=F32), GRID_W)
    col = jnp.tile(jnp.arange(GRID_W, dtype=F32), rows)
    quarter = dim // 4
    inv = 1.0 / (ROPE_BASE ** (jnp.arange(quarter, dtype=F32) / quarter))
    ang = jnp.concatenate([row[:, None] * inv, col[:, None] * inv], axis=-1)
    return jnp.cos(ang), jnp.sin(ang)


def _rope_tables():
    t = DEC_SEQ
    cos_h, sin_h = _rope_angles(t, HEAD_DIM)
    c64 = jnp.tile(cos_h, (1, 4))
    s64 = jnp.concatenate([-sin_h, -sin_h, sin_h, sin_h], axis=1)
    cos_r, sin_r = _rope_angles(t, MLA_ROPE_DIM)
    fill = LANES // 2 - MLA_ROPE_DIM // 2
    ones, zeros = jnp.ones((t, fill), F32), jnp.zeros((t, fill), F32)
    cm = jnp.concatenate([cos_r, ones, cos_r, ones], axis=1)
    sm = jnp.concatenate([-sin_r, zeros, sin_r, zeros], axis=1)
    return jnp.stack([c64, s64, cm, sm])


def _interleave_pairs(w):
    lead = w.shape[:-1]
    w = w.reshape(lead + (-1, 2, 2, HEAD_DIM // 2))
    return jnp.swapaxes(w, -2, -3).reshape(lead + (-1,))


def _place96(x):
    half = MLA_ROPE_DIM // 2
    cut = LANES // 2 - half
    pad = jnp.zeros(x.shape[:-1] + (LANES - MLA_QK_DIM,), x.dtype)
    return jnp.concatenate(
        [x[..., MLA_NOPE_DIM:MLA_NOPE_DIM + half], x[..., :cut], x[..., MLA_NOPE_DIM + half:],
         x[..., cut:MLA_NOPE_DIM], pad], axis=-1)


def _place_nope(x):
    rope = jnp.zeros(x.shape[:-1] + (MLA_ROPE_DIM,), x.dtype)
    return _place96(jnp.concatenate([x, rope], axis=-1))


def _place_rope(x):
    nope = jnp.zeros(x.shape[:-1] + (MLA_NOPE_DIM,), x.dtype)
    return _place96(jnp.concatenate([nope, x], axis=-1))


def _unplace_rope(x):
    half = MLA_ROPE_DIM // 2
    return jnp.concatenate([x[..., :half], x[..., LANES // 2:LANES // 2 + half]], axis=-1)


def _heads96(v):
    return jnp.tile(_place96(v), MLA_HEADS)


def _vec_pack(l, lam_init, interleaved, norm_mix, norm_ffn, diff_qk_norm, mla_q_norm, mla_kv_norm,
              mla_qk_norm, gqa_qk_norm, diff_out_norm):
    pair = _interleave_pairs if interleaved else (lambda v: v)

    def row(v):
        return jnp.pad(v.astype(F32), (0, D_MODEL - v.shape[0]))
    rows = [None] * N_VEC_ROWS
    rows[V_NORM_MIX] = row(norm_mix[l])
    rows[V_NORM_FFN] = row(norm_ffn[l])
    rows[V_AQ] = row(pair(jnp.tile(diff_qk_norm[l, 0], 8)) * HEAD_DIM ** -0.5 * LOG2E)
    rows[V_AK] = row(pair(jnp.tile(diff_qk_norm[l, 1], 8)))
    rows[V_BQD] = row(mla_q_norm[l])
    rows[V_BQ] = row(_heads96(mla_qk_norm[l, 0]) * MLA_QK_DIM ** -0.5 * LOG2E)
    rows[V_CKV] = row(mla_kv_norm[l])
    rows[V_BK] = row(_heads96(mla_qk_norm[l, 1]))
    rows[V_CQ] = row(pair(jnp.tile(gqa_qk_norm[l, 0], 8)) * HEAD_DIM ** -0.5 * LOG2E)
    rows[V_CK] = row(pair(jnp.tile(gqa_qk_norm[l, 1], 4)))
    rows[V_AOUT] = row(diff_out_norm[l] * (1.0 - lam_init))
    zero = jnp.zeros((D_MODEL,), F32)
    return jnp.stack([r if r is not None else zero for r in rows])


def _dup_heads64(w, n_heads):
    lead = w.shape[:-1]
    w = w.reshape(lead + (n_heads, 1, HEAD_DIM))
    return jnp.broadcast_to(w, lead + (n_heads, 2, HEAD_DIM)).reshape(lead + (n_heads * LANES,))


def _w1_layout(w_in_l, interleaved):
    pair = _interleave_pairs if interleaved else (lambda w: w)
    aq, ak, av = w_in_l[:, 0:512], w_in_l[:, 512:1024], w_in_l[:, 1024:1536]
    bqd = w_in_l[:, 1536:1792]
    ckv = w_in_l[:, 1792:1920]
    kpe = w_in_l[:, 1920:1952]
    cq = w_in_l[:, 1952:2464]
    ck = w_in_l[:, 2464:2592]
    cv = w_in_l[:, 2592:2720]
    w1 = jnp.concatenate(
        [pair(aq), pair(ak), av, bqd, ckv, _place_rope(kpe), pair(cq),
         pair(_dup_heads64(ck, GQA_KV_HEADS)), _dup_heads64(cv, GQA_KV_HEADS)], axis=1)
    return w1.astype(BF16)


def _wq_layout(w_q_up_l):
    w = _place96(w_q_up_l.reshape(MLA_Q_RANK, MLA_HEADS, MLA_QK_DIM))
    return w.reshape(MLA_Q_RANK, MLA_HEADS * LANES).astype(BF16)


def _wkv_layout(w_kv_up_l):
    w = w_kv_up_l.reshape(MLA_KV_RANK, MLA_HEADS, MLA_NOPE_DIM + MLA_V_DIM)
    k = _place_nope(w[:, :, :MLA_NOPE_DIM])
    v = w[:, :, MLA_NOPE_DIM:]
    return jnp.concatenate(
        [k.reshape(MLA_KV_RANK, MLA_HEADS * LANES), v.reshape(MLA_KV_RANK, MLA_HEADS * MLA_V_DIM)],
        axis=1).astype(BF16)


def _groups(x, n_groups):
    b, l, s, _ = x.shape
    return jnp.transpose(x.reshape(b, l, s, n_groups, LANES), (1, 0, 3, 2, 4)).astype(BF16)


def kernel(x_prompt, x_sample, c, cache_diff_k, cache_diff_v, cache_mla_ckv, cache_mla_kpe,
           cache_gqa_k, cache_gqa_v, c_ctx, w_ada, b_ada, norm_mix, w_in, diff_qk_norm,
           diff_lambda, diff_out_norm, mla_q_norm, w_mla_q_up, mla_kv_norm, w_mla_kv_up,
           mla_qk_norm, gqa_qk_norm, w_branch_a, w_branch_b, w_branch_c, w_o, norm_ffn,
           w_up, conv_w, conv_b, w_down):
    lam_inits = [0.8 - 0.6 * math.exp(-0.3 * l) for l in range(DEPTH)]
    vecs, vecs_lat = (jnp.stack([
        _vec_pack(l, lam_inits[l], interleaved, norm_mix, norm_ffn, diff_qk_norm, mla_q_norm,
                  mla_kv_norm, mla_qk_norm, gqa_qk_norm, diff_out_norm) for l in range(DEPTH)])
        for interleaved in (False, True))
    wkv_all = jnp.stack([_wkv_layout(w_mla_kv_up[l]) for l in range(DEPTH)])

    cvec = jnp.concatenate(
        [c_ctx[None, :], c, jnp.zeros((N_MOD_ROWS - 1 - DEC_BATCH, D_MODEL), F32)], axis=0)
    mods = _mods(cvec, w_ada, b_ada).reshape(DEPTH, N_MOD_ROWS, 6, D_MODEL)

    rope_tab = _rope_tables()

    ctx_bk, ctx_bv = _ctx_mla(cache_mla_ckv, _place_rope(cache_mla_kpe), wkv_all, vecs)
    nb, nl, ns = DEC_BATCH, DEPTH, PAST_LEN
    ctx = {
        "ak": _groups(_interleave_pairs(cache_diff_k.reshape(nb, nl, ns, 512)), 4),
        "av": _groups(cache_diff_v.reshape(nb, nl, ns, 512), 4),
        "bk": ctx_bk,
        "bv": ctx_bv,
        "ck": _groups(_interleave_pairs(_dup_heads64(cache_gqa_k.reshape(nb, nl, ns, 128), GQA_KV_HEADS)), 2),
        "cv": _groups(_dup_heads64(cache_gqa_v.reshape(nb, nl, ns, 128), GQA_KV_HEADS), 2),
    }

    y_p = x_prompt.reshape(BATCH * SEQ, D_MODEL)
    y_s = x_sample.reshape(DEC_BATCH * DEC_SEQ, D_MODEL)
    states = tuple(jnp.zeros((BATCH, DEPTH, SEQ, w), F32) for w in STATE_WIDTHS)
    for l in range(DEPTH):
        w1, w1_lat = _w1_layout(w_in[l], False), _w1_layout(w_in[l], True)
        wg = w_in[l][:, IN_GATES_COL:].astype(BF16)
        wq = _wq_layout(w_mla_q_up[l])
        wkv = wkv_all[l]
        wa, wb, wc = (w.astype(BF16) for w in (w_branch_a[l], w_branch_b[l], w_branch_c[l]))
        wo = w_o[l].astype(BF16)
        nch = 2 * N_FF_CHUNKS
        wu = w_up[l].astype(BF16)
        wd = w_down[l].astype(BF16).reshape(N_FF_CHUNKS, TF, D_MODEL)
        cw = jnp.transpose(conv_w[l].reshape(3, nch, TF), (1, 0, 2))
        cb = conv_b[l].reshape(nch, 1, TF)
        vec_l, mod_l, lam_l = vecs[l], mods[l], diff_lambda[l]

        qkv, *states = _inproj(y_p, mod_l, vec_l, w1, wq, wkv, None, latent=False, layer=l, states=states)
        oa, ob, oc = _attention(qkv, l, lam_inits[l], lam_l, vec_l, None, latent=False)
        y_p = _merge(y_p, mod_l, vec_l, oa, ob, oc, wg, wa, wb, wc, wo, latent=False)
        y_p = _ffn(y_p, mod_l, vec_l, wu, cw, cb, wd, latent=False)

        qkv = _inproj(y_s, mod_l, vecs_lat[l], w1_lat, wq, wkv, rope_tab, latent=True)
        oa, ob, oc = _attention(qkv, l, lam_inits[l], lam_l, vec_l, ctx, latent=True)
        y_s = _merge(y_s, mod_l, vec_l, oa, ob, oc, wg, wa, wb, wc, wo, latent=True)
        y_s = _ffn(y_s, mod_l, vec_l, wu, cw, cb, wd, latent=True)

    st_ak, st_av, st_ckv, st_kpe, st_ck, st_cv = states
    lead = (BATCH, DEPTH, SEQ)
    return (
        y_p.reshape(BATCH, SEQ, D_MODEL),
        y_s.reshape(DEC_BATCH, DEC_SEQ, D_MODEL),
        st_ak.reshape(lead + (2, DIFF_HEADS, HEAD_DIM)),
        st_av.reshape(lead + (DIFF_HEADS, DIFF_V_DIM)),
        st_ckv,
        _unplace_rope(st_kpe),
        st_ck.reshape(lead + (GQA_KV_HEADS, HEAD_DIM)),
        st_cv.reshape(lead + (GQA_KV_HEADS, HEAD_DIM)),
    )
```

```python
import functools
import math

import jax
import jax.numpy as jnp
import numpy as np
from jax import lax
from jax.experimental import pallas as pl
from jax.experimental.pallas import tpu as pltpu

D_MODEL = 1024
BATCH = 32
SEQ = 256
DEPTH = 4
DEC_BATCH = 4
DEC_SEQ = 2048
PAST_LEN = 256
GRID_W = 64
HEAD_DIM = 64
DIFF_HEADS = 4
DIFF_V_DIM = 2 * HEAD_DIM
MLA_HEADS = 8
MLA_Q_RANK = 256
MLA_KV_RANK = 128
MLA_NOPE_DIM = 64
MLA_ROPE_DIM = 32
MLA_QK_DIM = MLA_NOPE_DIM + MLA_ROPE_DIM
MLA_V_DIM = 64
GQA_HEADS = 8
GQA_KV_HEADS = 2
D_FF = 2816
ROPE_BASE = 10000.0
NORM_EPS = 1e-6
LOG2E = math.log2(math.e)

LANES = 128
MXU_N = 256
VMEM_LIMIT_BYTES = 56 * 1024 * 1024

F32 = jnp.float32
BF16 = jnp.bfloat16

C_AQ, C_AK, C_AV = 0, 512, 1024
C_BQD, C_CKV, C_KPE = 1536, 1792, 1920
C_CQ, C_CK, C_CV = 2048, 2560, 2816
W1_COLS = 3072
IN_GATES_COL = 2720

G_BQ, G_BK, G_AQ, G_AK, G_AV, G_BV, G_CQ, G_CK, G_CV = 0, 8, 16, 20, 24, 28, 32, 36, 38
N_GROUPS = 40
STATE_WIDTHS = (512, 512, LANES, LANES, LANES, LANES)
N_STATES = len(STATE_WIDTHS)

V_NORM_MIX, V_NORM_FFN, V_AQ, V_AK, V_BQD, V_BQ, V_CKV, V_BK, V_CQ, V_CK, V_AOUT = range(11)
N_VEC_ROWS = 16

M_SHIFT_M, M_SCALE_M, M_GATE_M, M_SHIFT_F, M_SCALE_F, M_GATE_F = range(6)
N_MOD_ROWS = 8

TM_IN = 256
TM_IN_LATENT = 512
TQ = 512
CTX_SEQS_PER_STEP = 4
TM_MERGE = 512
TM_FFN = 1024
FFN_CONV_ROWS = 128
FFN_HALO = 16
TF = 256
N_FF_CHUNKS = D_FF // TF
FFN_SLOTS = 3
TN_ADA = 1536


def _cparams(sem, flags=None):
    return pltpu.CompilerParams(dimension_semantics=sem, vmem_limit_bytes=VMEM_LIMIT_BYTES, flags=flags)


def _rms(x, gain):
    ms = jnp.mean(x * x, axis=-1, keepdims=True)
    return x * lax.rsqrt(ms + NORM_EPS) * gain


def _lane_lo(rows):
    return lax.broadcasted_iota(jnp.int32, (rows, LANES), 1) < HEAD_DIM


def _first_head_lanes(rows):
    lane = lax.broadcasted_iota(jnp.int32, (rows, LANES), 1)
    return (lane & (HEAD_DIM // 2)) == 0


def _same_head_matrix():
    r = lax.broadcasted_iota(jnp.int32, (LANES, LANES), 0)
    c = lax.broadcasted_iota(jnp.int32, (LANES, LANES), 1)
    same = (r & (HEAD_DIM // 2)) == (c & (HEAD_DIM // 2))
    return jnp.where(same, 1.0, 0.0).astype(BF16)


def _head_norms_lane_sum(groups, gains, width):
    outs = []
    for g, gain in zip(groups, gains):
        ssq = jnp.sum(g * g, axis=-1, keepdims=True)
        outs.append(g * lax.rsqrt(ssq * (1.0 / width) + NORM_EPS) * gain)
    return outs


def _pair_norms_lane_sum(groups, gains, lo):
    outs = []
    for g, gain in zip(groups, gains):
        sq = g * g
        s_all = jnp.sum(sq, axis=-1, keepdims=True)
        s_lo = jnp.sum(jnp.where(lo, sq, 0.0), axis=-1, keepdims=True)
        ssq = jnp.where(lo, s_lo, s_all - s_lo)
        outs.append(g * lax.rsqrt(ssq * (1.0 / HEAD_DIM) + NORM_EPS) * gain)
    return outs


def _head_norms(groups, gains, ones_mat, width):
    rows = groups[0].shape[0]
    sq = jnp.concatenate([(g * g).astype(BF16) for g in groups], axis=0)
    ssq = jnp.dot(sq, ones_mat, preferred_element_type=F32)
    outs = []
    for i, (g, gain) in enumerate(zip(groups, gains)):
        s = ssq[i * rows:(i + 1) * rows]
        outs.append(g * lax.rsqrt(s * (1.0 / width) + NORM_EPS) * gain)
    return outs


def _rope(xg, c, s):
    return xg * c + pltpu.roll(xg, LANES // 2, 1) * s


def _sigmoid(z):
    return 1.0 / (1.0 + jnp.exp(-z))


def _half_heads(qg, lo):
    qf = qg.astype(F32)
    return jnp.where(lo, qf, 0.0).astype(BF16), jnp.where(lo, 0.0, qf).astype(BF16)


def _mods_kernel(c_ref, w_ref, b_ref, o_ref):
    c = c_ref[...]
    a = c * _sigmoid(c)
    w = w_ref[...].astype(BF16)
    a_hi = a.astype(BF16)
    a_lo = (a - a_hi.astype(F32)).astype(BF16)
    acc = jnp.dot(a_hi, w, preferred_element_type=F32) + jnp.dot(a_lo, w, preferred_element_type=F32)
    o_ref[...] = acc + b_ref[...]


def _mods(cvec, w_ada, b_ada):
    n = 6 * D_MODEL
    return pl.pallas_call(
        _mods_kernel,
        out_shape=jax.ShapeDtypeStruct((DEPTH, N_MOD_ROWS, n), F32),
        grid=(DEPTH, n // TN_ADA),
        in_specs=[
            pl.BlockSpec((N_MOD_ROWS, D_MODEL), lambda l, j: (0, 0)),
            pl.BlockSpec((None, D_MODEL, TN_ADA), lambda l, j: (l, 0, j)),
            pl.BlockSpec((None, 1, TN_ADA), lambda l, j: (l, 0, j)),
        ],
        out_specs=pl.BlockSpec((None, N_MOD_ROWS, TN_ADA), lambda l, j: (l, 0, j)),
        compiler_params=_cparams(("arbitrary", "arbitrary")),
        name="ada_mods",
    )(cvec, w_ada, b_ada.reshape(DEPTH, 1, n))


def _mla_head_norms(groups, gains, on_mxu):
    if on_mxu:
        return _head_norms(groups, gains, jnp.ones((LANES, LANES), BF16), MLA_QK_DIM)
    return _head_norms_lane_sum(groups, gains, MLA_QK_DIM)


def _mla_expand(ckv, kpe_grp, wkv_ref, vec_ref, on_mxu=False):
    ckvn = _rms(ckv, vec_ref[V_CKV:V_CKV + 1, 0:MLA_KV_RANK]).astype(BF16)
    kv = jnp.dot(ckvn, wkv_ref[...], preferred_element_type=F32)
    kpre = [kv[:, h * LANES:(h + 1) * LANES] + kpe_grp for h in range(MLA_HEADS)]
    gains = [vec_ref[V_BK:V_BK + 1, h * LANES:(h + 1) * LANES] for h in range(MLA_HEADS)]
    return _mla_head_norms(kpre, gains, on_mxu), kv[:, MLA_HEADS * LANES:]


def _ctx_mla_kernel(ckv_ref, kpe_ref, wkv_ref, vec_ref, k_ref, v_ref):
    keys, vals = _mla_expand(ckv_ref[...], kpe_ref[...], wkv_ref, vec_ref)
    for h in range(MLA_HEADS):
        k_ref[h] = keys[h].astype(BF16)
    for p in range(MLA_HEADS // 2):
        v_ref[p] = vals[:, p * LANES:(p + 1) * LANES].astype(BF16)


def _ctx_mla(cache_ckv, cache_kpe_grp, wkv, vecs):
    return pl.pallas_call(
        _ctx_mla_kernel,
        out_shape=(
            jax.ShapeDtypeStruct((DEPTH, DEC_BATCH, MLA_HEADS, PAST_LEN, LANES), BF16),
            jax.ShapeDtypeStruct((DEPTH, DEC_BATCH, MLA_HEADS // 2, PAST_LEN, LANES), BF16),
        ),
        grid=(DEPTH, DEC_BATCH),
        in_specs=[
            pl.BlockSpec((None, None, PAST_LEN, MLA_KV_RANK), lambda l, b: (b, l, 0, 0)),
            pl.BlockSpec((None, None, PAST_LEN, LANES), lambda l, b: (b, l, 0, 0)),
            pl.BlockSpec((None, MLA_KV_RANK, 12 * LANES), lambda l, b: (l, 0, 0)),
            pl.BlockSpec((None, N_VEC_ROWS, D_MODEL), lambda l, b: (l, 0, 0)),
        ],
        out_specs=(
            pl.BlockSpec((None, None, MLA_HEADS, PAST_LEN, LANES), lambda l, b: (l, b, 0, 0, 0)),
            pl.BlockSpec((None, None, MLA_HEADS // 2, PAST_LEN, LANES), lambda l, b: (l, b, 0, 0, 0)),
        ),
        compiler_params=_cparams(("arbitrary", "arbitrary")),
        name="ctx_mla_expand",
    )(cache_ckv, cache_kpe_grp, wkv, vecs)


def _inproj_kernel(latent, x_ref, mod_ref, vec_ref, w1_ref, wq_ref, wkv_ref, *rest):
    if latent:
        rope_ref, qkv_ref, p_scr = rest
    else:
        qkv_ref, st_ak, st_av, st_ckv, st_kpe, st_ck, st_cv, p_scr = rest[N_STATES:]
    rows = x_ref.shape[0]
    lo = _first_head_lanes(rows)

    x = x_ref[...]
    h = _rms(x, vec_ref[V_NORM_MIX:V_NORM_MIX + 1, :])
    h = h * (1.0 + mod_ref[M_SCALE_M:M_SCALE_M + 1, :]) + mod_ref[M_SHIFT_M:M_SHIFT_M + 1, :]
    p_scr[...] = jnp.dot(h.astype(BF16), w1_ref[...], preferred_element_type=F32)

    def grp(col, g):
        return p_scr[:, col + g * LANES:col + (g + 1) * LANES]

    def vec(row, g):
        return vec_ref[row:row + 1, g * LANES:(g + 1) * LANES]

    def rope64(y):
        if not latent:
            return y
        return _rope(y, rope_ref[0], rope_ref[1])

    def rope_tail(y):
        if not latent:
            return y
        return _rope(y, rope_ref[2], rope_ref[3])

    sections = ((C_AQ, V_AQ, 4), (C_AK, V_AK, 4), (C_CQ, V_CQ, 4), (C_CK, V_CK, 2))
    raw = [grp(col, g) for col, _, n in sections for g in range(n)]
    gains = [vec(row, g) for _, row, n in sections for g in range(n)]
    if latent:
        normed = _head_norms(raw, gains, _same_head_matrix(), HEAD_DIM)
    else:
        normed = _pair_norms_lane_sum(raw, gains, lo)
    aqn, akn, cqn, ckn = normed[0:4], normed[4:8], normed[8:12], normed[12:14]
    for g in range(4):
        qkv_ref[G_AQ + g] = rope64(aqn[g]).astype(BF16)
        qkv_ref[G_AK + g] = rope64(akn[g]).astype(BF16)
        qkv_ref[G_CQ + g] = rope64(cqn[g]).astype(BF16)
        av = grp(C_AV, g)
        qkv_ref[G_AV + g] = av.astype(BF16)
        if not latent:
            st_ak[:, g * LANES:(g + 1) * LANES] = akn[g]
            st_av[:, g * LANES:(g + 1) * LANES] = av
    cvs = [grp(C_CV, g) for g in range(2)]
    for g in range(2):
        qkv_ref[G_CK + g] = rope64(ckn[g]).astype(BF16)
        qkv_ref[G_CV + g] = cvs[g].astype(BF16)
    if not latent:
        st_ck[...] = jnp.where(lo, ckn[0], ckn[1])
        st_cv[...] = jnp.where(_lane_lo(rows), cvs[0], cvs[1])

    bqn = _rms(p_scr[:, C_BQD:C_BQD + MLA_Q_RANK], vec_ref[V_BQD:V_BQD + 1, 0:MLA_Q_RANK]).astype(BF16)
    bq = jnp.dot(bqn, wq_ref[...], preferred_element_type=F32)
    qn = _mla_head_norms([bq[:, hd * LANES:(hd + 1) * LANES] for hd in range(MLA_HEADS)],
                         [vec(V_BQ, hd) for hd in range(MLA_HEADS)], on_mxu=latent)
    ckv = p_scr[:, C_CKV:C_CKV + MLA_KV_RANK]
    kpe_grp = p_scr[:, C_KPE:C_KPE + LANES]
    keys, vals = _mla_expand(ckv, kpe_grp, wkv_ref, vec_ref, on_mxu=latent)
    for hd in range(MLA_HEADS):
        qkv_ref[G_BQ + hd] = rope_tail(qn[hd]).astype(BF16)
        qkv_ref[G_BK + hd] = rope_tail(keys[hd]).astype(BF16)
    for p in range(MLA_HEADS // 2):
        qkv_ref[G_BV + p] = vals[:, p * LANES:(p + 1) * LANES].astype(BF16)
    if not latent:
        st_ckv[...] = ckv
        st_kpe[...] = kpe_grp


def _inproj(x, mod_l, vec_l, w1, wq, wkv, rope_tab, latent, layer=None, states=None):
    t = x.shape[0]
    tm = TM_IN_LATENT if latent else TM_IN
    n_tiles = t // tm
    tiles_per_seq = DEC_SEQ // tm
    if latent:
        mod_idx = lambda i: (1 + i // tiles_per_seq, 0, 0)
    else:
        mod_idx = lambda i: (0, 0, 0)
    const2 = lambda i: (0, 0)
    in_specs = [
        pl.BlockSpec((tm, D_MODEL), lambda i: (i, 0)),
        pl.BlockSpec((None, 6, D_MODEL), mod_idx),
        pl.BlockSpec((N_VEC_ROWS, D_MODEL), const2),
        pl.BlockSpec((D_MODEL, W1_COLS), const2),
        pl.BlockSpec((MLA_Q_RANK, MLA_HEADS * LANES), const2),
        pl.BlockSpec((MLA_KV_RANK, 12 * LANES), const2),
    ]
    args = [x, mod_l, vec_l, w1, wq, wkv]
    qkv_shape = jax.ShapeDtypeStruct((N_GROUPS, t, LANES), BF16)
    qkv_spec = pl.BlockSpec((N_GROUPS, tm, LANES), lambda i: (0, i, 0))
    if latent:
        in_specs.append(pl.BlockSpec((4, tm, LANES), lambda i: (0, i % tiles_per_seq, 0)))
        args.append(rope_tab)
        out_shape = qkv_shape
        out_specs = qkv_spec
    else:
        assert TM_IN == SEQ
        aliases = {len(args) + j: 1 + j for j in range(N_STATES)}
        in_specs += [pl.BlockSpec(memory_space=pl.ANY)] * N_STATES
        args += list(states)
        out_shape = (qkv_shape,) + tuple(jax.ShapeDtypeStruct(st.shape, F32) for st in states)
        out_specs = (qkv_spec,) + tuple(
            pl.BlockSpec((None, None, SEQ, st.shape[-1]), lambda i: (i, layer, 0, 0)) for st in states)
    return pl.pallas_call(
        functools.partial(_inproj_kernel, latent),
        out_shape=out_shape,
        grid=(n_tiles,),
        in_specs=in_specs,
        out_specs=out_specs,
        scratch_shapes=[pltpu.VMEM((tm, W1_COLS), F32)],
        input_output_aliases={} if latent else aliases,
        compiler_params=_cparams(("arbitrary",)),
        name="inproj_latent" if latent else "inproj_context",
    )(*args)


def _attend(q, k_parts, v_parts):
    nt = (((1,), (1,)), ((), ()))
    ss = [lax.dot_general(q, k, nt, preferred_element_type=F32) for k in k_parts]
    m = functools.reduce(jnp.maximum, [jnp.max(s, axis=-1, keepdims=True) for s in ss])
    ps = [jnp.exp2(s - m) for s in ss]
    l = functools.reduce(jnp.add, [jnp.sum(p, axis=-1, keepdims=True) for p in ps])
    o = functools.reduce(
        jnp.add,
        [jnp.dot(p.astype(BF16), v, preferred_element_type=F32) for p, v in zip(ps, v_parts)])
    return o / l


def _seq_rows(ref, n_seq, sq):
    n = ref.shape[1] // n_seq
    return pl.ds(sq * n, n)


def _mla_attn_kernel(has_ctx, n_seq, q_ref, k_ref, v_ref, *rest):
    if has_ctx:
        kc_ref, vc_ref, o_ref = rest
    else:
        (o_ref,) = rest
    lo_out = _lane_lo(q_ref.shape[1] // n_seq)
    for sq in range(n_seq):
        rq, rk = _seq_rows(q_ref, n_seq, sq), _seq_rows(k_ref, n_seq, sq)
        for u in range(MLA_HEADS // 2):
            outs = []
            for half in range(2):
                hd = 2 * u + half
                ks = ([kc_ref[hd]] if has_ctx else []) + [k_ref[hd, rk, :]]
                vs = ([vc_ref[u]] if has_ctx else []) + [v_ref[u, rk, :]]
                outs.append(_attend(q_ref[hd, rq, :], ks, vs))
            o_ref[u, rq, :] = jnp.where(lo_out, outs[0], outs[1]).astype(BF16)


def _gqa_attn_kernel(has_ctx, n_seq, q_ref, k_ref, v_ref, *rest):
    if has_ctx:
        kc_ref, vc_ref, o_ref = rest
    else:
        (o_ref,) = rest
    tq = q_ref.shape[1] // n_seq
    lo_out = _lane_lo(tq)
    lo_q = _first_head_lanes(tq)
    for sq in range(n_seq):
        rq, rk = _seq_rows(q_ref, n_seq, sq), _seq_rows(k_ref, n_seq, sq)
        for u in range(GQA_HEADS // 2):
            kv = u // 2
            ks = ([kc_ref[kv]] if has_ctx else []) + [k_ref[kv, rk, :]]
            vs = ([vc_ref[kv]] if has_ctx else []) + [v_ref[kv, rk, :]]
            q_a, q_b = _half_heads(q_ref[u, rq, :], lo_q)
            o_ref[u, rq, :] = jnp.where(lo_out, _attend(q_a, ks, vs), _attend(q_b, ks, vs)).astype(BF16)


def _diff_attn_kernel(has_ctx, n_seq, lam_init, q_ref, k_ref, v_ref, lam_ref, vec_ref, *rest):
    if has_ctx:
        kc_ref, vc_ref, o_ref = rest
    else:
        (o_ref,) = rest
    lo_q = _first_head_lanes(q_ref.shape[1] // n_seq)
    lv = lam_ref[...]
    lam = (jnp.exp(jnp.sum(lv[0:1] * lv[1:2], axis=-1, keepdims=True))
           - jnp.exp(jnp.sum(lv[2:3] * lv[3:4], axis=-1, keepdims=True)) + lam_init)
    gain = vec_ref[V_AOUT:V_AOUT + 1, 0:LANES]
    for sq in range(n_seq):
        rq, rk = _seq_rows(q_ref, n_seq, sq), _seq_rows(k_ref, n_seq, sq)
        for hd in range(DIFF_HEADS):
            pp, half = hd // 2, hd % 2
            vs = ([vc_ref[hd]] if has_ctx else []) + [v_ref[hd, rk, :]]
            os = []
            for m in range(2):
                g = 2 * m + pp
                q = _half_heads(q_ref[g, rq, :], lo_q)[half]
                ks = ([kc_ref[g]] if has_ctx else []) + [k_ref[g, rk, :]]
                os.append(_attend(q, ks, vs))
            o_ref[hd, rq, :] = _rms(os[0] - lam * os[1], gain).astype(BF16)


def _attn_specs(latent, groups_q, blk_q, groups_k, blk_k, groups_v, blk_v):
    if latent:
        per = DEC_SEQ // TQ
        q_idx = lambda b, j: (blk_q, b * per + j, 0)
        k_idx = lambda b, j: (blk_k, b, 0)
        v_idx = lambda b, j: (blk_v, b, 0)
        rows_q, rows_k = TQ, DEC_SEQ
    else:
        q_idx = lambda i: (blk_q, i, 0)
        k_idx = lambda i: (blk_k, i, 0)
        v_idx = lambda i: (blk_v, i, 0)
        rows_q = rows_k = CTX_SEQS_PER_STEP * SEQ
    return [
        pl.BlockSpec((groups_q, rows_q, LANES), q_idx),
        pl.BlockSpec((groups_k, rows_k, LANES), k_idx),
        pl.BlockSpec((groups_v, rows_k, LANES), v_idx),
    ]


def _ctx_spec(groups, layer):
    return pl.BlockSpec((None, None, groups, PAST_LEN, LANES), lambda b, j: (layer, b, 0, 0, 0))


def _attn_call(kernel, name, latent, specs, args, t):
    if latent:
        grid = (DEC_BATCH, DEC_SEQ // TQ)
        o_idx = lambda b, j: (0, b * (DEC_SEQ // TQ) + j, 0)
        sem = ("arbitrary", "arbitrary")
        rows = TQ
    else:
        rows = CTX_SEQS_PER_STEP * SEQ
        grid = (t // rows,)
        o_idx = lambda i: (0, i, 0)
        sem = ("arbitrary",)
    return pl.pallas_call(
        kernel,
        out_shape=jax.ShapeDtypeStruct((4, t, LANES), BF16),
        grid=grid,
        in_specs=specs,
        out_specs=pl.BlockSpec((4, rows, LANES), o_idx),
        compiler_params=_cparams(sem),
        name=name + ("_latent" if latent else "_context"),
    )(*args)


def _attention(qkv, layer, lam_init, lam_l, vec_l, ctx, latent):
    t = qkv.shape[1]
    n_seq = 1 if latent else CTX_SEQS_PER_STEP
    if latent:
        const2 = lambda b, j: (0, 0)
    else:
        const2 = lambda i: (0, 0)

    specs = _attn_specs(latent, 4, G_AQ // 4, 4, G_AK // 4, 4, G_AV // 4)
    specs += [pl.BlockSpec((4, HEAD_DIM), const2), pl.BlockSpec((N_VEC_ROWS, D_MODEL), const2)]
    args = [qkv, qkv, qkv, lam_l, vec_l]
    if latent:
        specs += [_ctx_spec(4, layer), _ctx_spec(4, layer)]
        args += [ctx["ak"], ctx["av"]]
    oa = _attn_call(functools.partial(_diff_attn_kernel, latent, n_seq, lam_init), "diff_attn", latent,
                    specs, args, t)

    specs = _attn_specs(latent, 8, G_BQ // 8, 8, G_BK // 8, 4, G_BV // 4)
    args = [qkv, qkv, qkv]
    if latent:
        specs += [_ctx_spec(8, layer), _ctx_spec(4, layer)]
        args += [ctx["bk"], ctx["bv"]]
    ob = _attn_call(functools.partial(_mla_attn_kernel, latent, n_seq), "mla_attn", latent, specs, args, t)

    specs = _attn_specs(latent, 4, G_CQ // 4, 2, G_CK // 2, 2, G_CV // 2)
    args = [qkv, qkv, qkv]
    if latent:
        specs += [_ctx_spec(2, layer), _ctx_spec(2, layer)]
        args += [ctx["ck"], ctx["cv"]]
    oc = _attn_call(functools.partial(_gqa_attn_kernel, latent, n_seq), "gqa_attn", latent, specs, args, t)
    return oa, ob, oc


def _merge_kernel(x_ref, mod_ref, vec_ref, oa_ref, ob_ref, oc_ref, wg_ref, wa_ref, wb_ref, wc_ref,
                  wo_ref, out_ref):
    x = x_ref[...]
    h = _rms(x, vec_ref[V_NORM_MIX:V_NORM_MIX + 1, :])
    h = h * (1.0 + mod_ref[M_SCALE_M:M_SCALE_M + 1, :]) + mod_ref[M_SHIFT_M:M_SHIFT_M + 1, :]
    hb = h.astype(BF16)
    merged = None
    for br, (o_ref, w_ref) in enumerate(((oa_ref, wa_ref), (ob_ref, wb_ref), (oc_ref, wc_ref))):
        o = jnp.concatenate([o_ref[g] for g in range(4)], axis=1)
        proj = jnp.dot(o, w_ref[...], preferred_element_type=F32)
        gate = _sigmoid(jnp.dot(hb, wg_ref[:, br * D_MODEL:(br + 1) * D_MODEL],
                                preferred_element_type=F32))
        merged = gate * proj if merged is None else merged + gate * proj
    mix = jnp.dot(merged.astype(BF16), wo_ref[...], preferred_element_type=F32)
    out_ref[...] = x + mod_ref[M_GATE_M:M_GATE_M + 1, :] * mix


def _merge(x, mod_l, vec_l, oa, ob, oc, wg, wa, wb, wc, wo, latent):
    t = x.shape[0]
    per = DEC_SEQ // TM_MERGE
    mod_idx = (lambda i: (1 + i // per, 0, 0)) if latent else (lambda i: (0, 0, 0))
    const2 = lambda i: (0, 0)
    o_spec = pl.BlockSpec((4, TM_MERGE, LANES), lambda i: (0, i, 0))
    width = 4 * LANES
    return pl.pallas_call(
        _merge_kernel,
        out_shape=jax.ShapeDtypeStruct((t, D_MODEL), F32),
        grid=(t // TM_MERGE,),
        in_specs=[
            pl.BlockSpec((TM_MERGE, D_MODEL), lambda i: (i, 0)),
            pl.BlockSpec((None, 6, D_MODEL), mod_idx),
            pl.BlockSpec((N_VEC_ROWS, D_MODEL), const2),
            o_spec, o_spec, o_spec,
            pl.BlockSpec((D_MODEL, 3 * D_MODEL), const2),
            pl.BlockSpec((width, D_MODEL), const2),
            pl.BlockSpec((width, D_MODEL), const2),
            pl.BlockSpec((width, D_MODEL), const2),
            pl.BlockSpec((D_MODEL, D_MODEL), const2),
        ],
        out_specs=pl.BlockSpec((TM_MERGE, D_MODEL), lambda i: (i, 0)),
        compiler_params=_cparams(("arbitrary",)),
        name="merge_latent" if latent else "merge_context",
    )(x, mod_l, vec_l, oa, ob, oc, wg, wa, wb, wc, wo)


def _zero_rows(a, zero_rows):
    if not zero_rows:
        return a
    sub = lax.broadcasted_iota(jnp.int32, (8, a.shape[1]), 0)
    pieces, cur = [], 0
    for r in sorted(zero_rows):
        base = (r // 8) * 8
        if base > cur:
            pieces.append(a[cur:base])
        pieces.append(jnp.where(sub == (r % 8), 0.0, a[base:base + 8]))
        cur = base + 8
    if cur < a.shape[0]:
        pieces.append(a[cur:])
    return jnp.concatenate(pieces, axis=0)


def _ffn_kernel(seq_len, x_ref, xp_ref, xn_ref, mod_ref, vec_ref, wu_ref, cw_ref, cb_ref, wd_ref,
                out_ref, h_scr, u_scr, act_scr):
    rows = x_ref.shape[0]
    ext = rows + 2 * FFN_HALO
    gain = vec_ref[V_NORM_FFN:V_NORM_FFN + 1, :]
    scale = 1.0 + mod_ref[M_SCALE_F:M_SCALE_F + 1, :]
    shift = mod_ref[M_SHIFT_F:M_SHIFT_F + 1, :]

    def hmod(x):
        return _rms(x, gain) * scale + shift

    row0 = pl.program_id(0) * rows
    keep_prev = (row0 & (seq_len - 1)) != 0
    keep_next = ((row0 + rows) & (seq_len - 1)) != 0
    h_scr[0:FFN_HALO, :] = jnp.where(keep_prev, hmod(xp_ref[...]), 0.0).astype(BF16)
    h_scr[FFN_HALO:FFN_HALO + rows, :] = hmod(x_ref[...]).astype(BF16)
    h_scr[FFN_HALO + rows:ext, :] = jnp.where(keep_next, hmod(xn_ref[...]), 0.0).astype(BF16)
    starts = list(range(seq_len, rows, seq_len))
    ends = [r - 1 for r in starts]

    def up(slot, k):
        for b in range(2):
            col = pl.multiple_of((b * N_FF_CHUNKS + k) * TF, TF)
            u_scr[slot, b] = jnp.dot(h_scr[...], wu_ref[:, pl.ds(col, TF)], preferred_element_type=F32)

    def conv(slot, k):
        pad = 8
        for r0 in range(0, rows, FFN_CONV_ROWS):
            blk_starts = [r - r0 for r in starts if r0 <= r < r0 + FFN_CONV_ROWS]
            blk_ends = [r - r0 for r in ends if r0 <= r < r0 + FFN_CONV_ROWS]

            def branch(b, j):
                lo_row = FFN_HALO + r0 - pad
                u = u_scr[slot, b, lo_row:lo_row + FFN_CONV_ROWS + 2 * pad, :]
                n = FFN_CONV_ROWS + 2 * pad
                prev = _zero_rows(pltpu.roll(u, 1, 0)[pad:pad + FFN_CONV_ROWS], blk_starts)
                nxt = _zero_rows(pltpu.roll(u, n - 1, 0)[pad:pad + FFN_CONV_ROWS], blk_ends)
                cur = u[pad:pad + FFN_CONV_ROWS]
                cw = cw_ref[j]
                return prev * cw[0:1, :] + cur * cw[1:2, :] + nxt * cw[2:3, :] + cb_ref[j]

            a = branch(0, k)
            g = branch(1, N_FF_CHUNKS + k)
            act_scr[slot, r0:r0 + FFN_CONV_ROWS, :] = (a * (g * _sigmoid(g))).astype(BF16)

    def down(slot, k):
        for n0 in range(0, D_MODEL, MXU_N):
            out_ref[:, n0:n0 + MXU_N] += jnp.dot(act_scr[slot], wd_ref[k, :, n0:n0 + MXU_N],
                                                 preferred_element_type=F32)

    def stage(k, k_mod):
        up(k_mod % FFN_SLOTS, k)
        conv((k_mod - 1) % FFN_SLOTS, k - 1)
        down((k_mod - 2) % FFN_SLOTS, k - 2)

    out_ref[...] = jnp.zeros_like(out_ref)
    up(0, 0)
    up(1, 1)
    conv(0, 0)

    def body(j, carry):
        k = 2 + FFN_SLOTS * j
        for i in range(FFN_SLOTS):
            stage(k + i, 2 + i)
        return carry

    last = N_FF_CHUNKS - 1
    n_trips = (last - 1) // FFN_SLOTS
    assert 2 + n_trips * FFN_SLOTS == last + 1
    lax.fori_loop(0, n_trips, body, 0)
    conv(last % FFN_SLOTS, last)
    down((last - 1) % FFN_SLOTS, last - 1)
    down(last % FFN_SLOTS, last)
    out_ref[...] = x_ref[...] + mod_ref[M_GATE_F:M_GATE_F + 1, :] * out_ref[...]


def _ffn(x, mod_l, vec_l, wu, cw, cb, wd, latent):
    t = x.shape[0]
    per = DEC_SEQ // TM_FFN
    mod_idx = (lambda i: (1 + i // per, 0, 0)) if latent else (lambda i: (0, 0, 0))
    seq_len = DEC_SEQ if latent else SEQ
    halo_per_tile = TM_FFN // FFN_HALO
    n_halo = t // FFN_HALO
    const3 = lambda i: (0, 0, 0)
    resident = pl.Buffered(1)
    return pl.pallas_call(
        functools.partial(_ffn_kernel, seq_len),
        out_shape=jax.ShapeDtypeStruct((t, D_MODEL), F32),
        grid=(t // TM_FFN,),
        in_specs=[
            pl.BlockSpec((TM_FFN, D_MODEL), lambda i: (i, 0)),
            pl.BlockSpec((FFN_HALO, D_MODEL), lambda i: (jnp.maximum(i * halo_per_tile - 1, 0), 0)),
            pl.BlockSpec((FFN_HALO, D_MODEL),
                         lambda i: (jnp.minimum((i + 1) * halo_per_tile, n_halo - 1), 0)),
            pl.BlockSpec((None, 6, D_MODEL), mod_idx),
            pl.BlockSpec((N_VEC_ROWS, D_MODEL), lambda i: (0, 0)),
            pl.BlockSpec((D_MODEL, 2 * D_FF), lambda i: (0, 0), pipeline_mode=resident),
            pl.BlockSpec((2 * N_FF_CHUNKS, 3, TF), const3),
            pl.BlockSpec((2 * N_FF_CHUNKS, 1, TF), const3),
            pl.BlockSpec((N_FF_CHUNKS, TF, D_MODEL), const3, pipeline_mode=resident),
        ],
        out_specs=pl.BlockSpec((TM_FFN, D_MODEL), lambda i: (i, 0)),
        scratch_shapes=[pltpu.VMEM((TM_FFN + 2 * FFN_HALO, D_MODEL), BF16),
                        pltpu.VMEM((FFN_SLOTS, 2, TM_FFN + 2 * FFN_HALO, TF), F32),
                        pltpu.VMEM((FFN_SLOTS, TM_FFN, TF), BF16)],
        compiler_params=_cparams(("arbitrary",)),
        name="ffn_latent" if latent else "ffn_context",
    )(x, x, x, mod_l, vec_l, wu, cw, cb, wd)


def _rope_angles(n_tokens, dim):
    rows = n_tokens // GRID_W
    row = jnp.repeat(jnp.arange(rows, dtype=F32), GRID_W)
    col = jnp.tile(jnp.arange(GRID_W, dtype=F32), rows)
    quarter = dim // 4
    inv = 1.0 / (ROPE_BASE ** (jnp.arange(quarter, dtype=F32) / quarter))
    ang = jnp.concatenate([row[:, None] * inv, col[:, None] * inv], axis=-1)
    return jnp.cos(ang), jnp.sin(ang)


def _rope_tables():
    t = DEC_SEQ
    cos_h, sin_h = _rope_angles(t, HEAD_DIM)
    c64 = jnp.tile(cos_h, (1, 4))
    s64 = jnp.concatenate([-sin_h, -sin_h, sin_h, sin_h], axis=1)
    cos_r, sin_r = _rope_angles(t, MLA_ROPE_DIM)
    fill = LANES // 2 - MLA_ROPE_DIM // 2
    ones, zeros = jnp.ones((t, fill), F32), jnp.zeros((t, fill), F32)
    cm = jnp.concatenate([cos_r, ones, cos_r, ones], axis=1)
    sm = jnp.concatenate([-sin_r, zeros, sin_r, zeros], axis=1)
    return jnp.stack([c64, s64, cm, sm])


def _interleave_pairs(w):
    lead = w.shape[:-1]
    w = w.reshape(lead + (-1, 2, 2, HEAD_DIM // 2))
    return jnp.swapaxes(w, -2, -3).reshape(lead + (-1,))


def _place96(x):
    half = MLA_ROPE_DIM // 2
    cut = LANES // 2 - half
    pad = jnp.zeros(x.shape[:-1] + (LANES - MLA_QK_DIM,), x.dtype)
    return jnp.concatenate(
        [x[..., MLA_NOPE_DIM:MLA_NOPE_DIM + half], x[..., :cut], x[..., MLA_NOPE_DIM + half:],
         x[..., cut:MLA_NOPE_DIM], pad], axis=-1)


def _place_nope(x):
    rope = jnp.zeros(x.shape[:-1] + (MLA_ROPE_DIM,), x.dtype)
    return _place96(jnp.concatenate([x, rope], axis=-1))


def _place_rope(x):
    nope = jnp.zeros(x.shape[:-1] + (MLA_NOPE_DIM,), x.dtype)
    return _place96(jnp.concatenate([nope, x], axis=-1))


def _unplace_rope(x):
    half = MLA_ROPE_DIM // 2
    return jnp.concatenate([x[..., :half], x[..., LANES // 2:LANES // 2 + half]], axis=-1)


def _heads96(v):
    return jnp.tile(_place96(v), MLA_HEADS)


def _vec_pack(l, lam_init, norm_mix, norm_ffn, diff_qk_norm, mla_q_norm, mla_kv_norm,
              mla_qk_norm, gqa_qk_norm, diff_out_norm):
    pair = _interleave_pairs

    def row(v):
        return jnp.pad(v.astype(F32), (0, D_MODEL - v.shape[0]))
    rows = [None] * N_VEC_ROWS
    rows[V_NORM_MIX] = row(norm_mix[l])
    rows[V_NORM_FFN] = row(norm_ffn[l])
    rows[V_AQ] = row(pair(jnp.tile(diff_qk_norm[l, 0], 8)) * HEAD_DIM ** -0.5 * LOG2E)
    rows[V_AK] = row(pair(jnp.tile(diff_qk_norm[l, 1], 8)))
    rows[V_BQD] = row(mla_q_norm[l])
    rows[V_BQ] = row(_heads96(mla_qk_norm[l, 0]) * MLA_QK_DIM ** -0.5 * LOG2E)
    rows[V_CKV] = row(mla_kv_norm[l])
    rows[V_BK] = row(_heads96(mla_qk_norm[l, 1]))
    rows[V_CQ] = row(pair(jnp.tile(gqa_qk_norm[l, 0], 8)) * HEAD_DIM ** -0.5 * LOG2E)
    rows[V_CK] = row(pair(jnp.tile(gqa_qk_norm[l, 1], 4)))
    rows[V_AOUT] = row(diff_out_norm[l] * (1.0 - lam_init))
    zero = jnp.zeros((D_MODEL,), F32)
    return jnp.stack([r if r is not None else zero for r in rows])


def _dup_heads64(w, n_heads):
    lead = w.shape[:-1]
    w = w.reshape(lead + (n_heads, 1, HEAD_DIM))
    return jnp.broadcast_to(w, lead + (n_heads, 2, HEAD_DIM)).reshape(lead + (n_heads * LANES,))


def _w1_layout(w_in_l):
    pair = _interleave_pairs
    aq, ak, av = w_in_l[:, 0:512], w_in_l[:, 512:1024], w_in_l[:, 1024:1536]
    bqd = w_in_l[:, 1536:1792]
    ckv = w_in_l[:, 1792:1920]
    kpe = w_in_l[:, 1920:1952]
    cq = w_in_l[:, 1952:2464]
    ck = w_in_l[:, 2464:2592]
    cv = w_in_l[:, 2592:2720]
    w1 = jnp.concatenate(
        [pair(aq), pair(ak), av, bqd, ckv, _place_rope(kpe), pair(cq),
         pair(_dup_heads64(ck, GQA_KV_HEADS)), _dup_heads64(cv, GQA_KV_HEADS)], axis=1)
    return w1.astype(BF16)


def _wq_layout(w_q_up_l):
    w = _place96(w_q_up_l.reshape(MLA_Q_RANK, MLA_HEADS, MLA_QK_DIM))
    return w.reshape(MLA_Q_RANK, MLA_HEADS * LANES).astype(BF16)


def _wkv_layout(w_kv_up_l):
    w = w_kv_up_l.reshape(MLA_KV_RANK, MLA_HEADS, MLA_NOPE_DIM + MLA_V_DIM)
    k = _place_nope(w[:, :, :MLA_NOPE_DIM])
    v = w[:, :, MLA_NOPE_DIM:]
    return jnp.concatenate(
        [k.reshape(MLA_KV_RANK, MLA_HEADS * LANES), v.reshape(MLA_KV_RANK, MLA_HEADS * MLA_V_DIM)],
        axis=1).astype(BF16)


def _groups(x, n_groups):
    b, l, s, _ = x.shape
    return jnp.transpose(x.reshape(b, l, s, n_groups, LANES), (1, 0, 3, 2, 4)).astype(BF16)


def kernel(x_prompt, x_sample, c, cache_diff_k, cache_diff_v, cache_mla_ckv, cache_mla_kpe,
           cache_gqa_k, cache_gqa_v, c_ctx, w_ada, b_ada, norm_mix, w_in, diff_qk_norm,
           diff_lambda, diff_out_norm, mla_q_norm, w_mla_q_up, mla_kv_norm, w_mla_kv_up,
           mla_qk_norm, gqa_qk_norm, w_branch_a, w_branch_b, w_branch_c, w_o, norm_ffn,
           w_up, conv_w, conv_b, w_down):
    lam_inits = [0.8 - 0.6 * math.exp(-0.3 * l) for l in range(DEPTH)]
    vecs = jnp.stack([
        _vec_pack(l, lam_inits[l], norm_mix, norm_ffn, diff_qk_norm, mla_q_norm, mla_kv_norm,
                  mla_qk_norm, gqa_qk_norm, diff_out_norm) for l in range(DEPTH)])
    wkv_all = jnp.stack([_wkv_layout(w_mla_kv_up[l]) for l in range(DEPTH)])

    cvec = jnp.concatenate(
        [c_ctx[None, :], c, jnp.zeros((N_MOD_ROWS - 1 - DEC_BATCH, D_MODEL), F32)], axis=0)
    mods = _mods(cvec, w_ada, b_ada).reshape(DEPTH, N_MOD_ROWS, 6, D_MODEL)

    rope_tab = _rope_tables()

    ctx_bk, ctx_bv = _ctx_mla(cache_mla_ckv, _place_rope(cache_mla_kpe), wkv_all, vecs)
    nb, nl, ns = DEC_BATCH, DEPTH, PAST_LEN
    ctx = {
        "ak": _groups(_interleave_pairs(cache_diff_k.reshape(nb, nl, ns, 512)), 4),
        "av": _groups(cache_diff_v.reshape(nb, nl, ns, 512), 4),
        "bk": ctx_bk,
        "bv": ctx_bv,
        "ck": _groups(_interleave_pairs(_dup_heads64(cache_gqa_k.reshape(nb, nl, ns, 128), GQA_KV_HEADS)), 2),
        "cv": _groups(_dup_heads64(cache_gqa_v.reshape(nb, nl, ns, 128), GQA_KV_HEADS), 2),
    }

    y_p = x_prompt.reshape(BATCH * SEQ, D_MODEL)
    y_s = x_sample.reshape(DEC_BATCH * DEC_SEQ, D_MODEL)
    states = tuple(jnp.zeros((BATCH, DEPTH, SEQ, w), F32) for w in STATE_WIDTHS)
    for l in range(DEPTH):
        w1 = _w1_layout(w_in[l])
        wg = w_in[l][:, IN_GATES_COL:].astype(BF16)
        wq = _wq_layout(w_mla_q_up[l])
        wkv = wkv_all[l]
        wa, wb, wc = (w.astype(BF16) for w in (w_branch_a[l], w_branch_b[l], w_branch_c[l]))
        wo = w_o[l].astype(BF16)
        nch = 2 * N_FF_CHUNKS
        wu = w_up[l].astype(BF16)
        wd = w_down[l].astype(BF16).reshape(N_FF_CHUNKS, TF, D_MODEL)
        cw = jnp.transpose(conv_w[l].reshape(3, nch, TF), (1, 0, 2))
        cb = conv_b[l].reshape(nch, 1, TF)
        vec_l, mod_l, lam_l = vecs[l], mods[l], diff_lambda[l]

        qkv, *states = _inproj(y_p, mod_l, vec_l, w1, wq, wkv, None, latent=False, layer=l, states=states)
        oa, ob, oc = _attention(qkv, l, lam_inits[l], lam_l, vec_l, None, latent=False)
        y_p = _merge(y_p, mod_l, vec_l, oa, ob, oc, wg, wa, wb, wc, wo, latent=False)
        y_p = _ffn(y_p, mod_l, vec_l, wu, cw, cb, wd, latent=False)

        qkv = _inproj(y_s, mod_l, vec_l, w1, wq, wkv, rope_tab, latent=True)
        oa, ob, oc = _attention(qkv, l, lam_inits[l], lam_l, vec_l, ctx, latent=True)
        y_s = _merge(y_s, mod_l, vec_l, oa, ob, oc, wg, wa, wb, wc, wo, latent=True)
        y_s = _ffn(y_s, mod_l, vec_l, wu, cw, cb, wd, latent=True)

    st_ak, st_av, st_ckv, st_kpe, st_ck, st_cv = states
    lead = (BATCH, DEPTH, SEQ)
    return (
        y_p.reshape(BATCH, SEQ, D_MODEL),
        y_s.reshape(DEC_BATCH, DEC_SEQ, D_MODEL),
        _interleave_pairs(st_ak).reshape(lead + (2, DIFF_HEADS, HEAD_DIM)),
        st_av.reshape(lead + (DIFF_HEADS, DIFF_V_DIM)),
        st_ckv,
        _unplace_rope(st_kpe),
        _interleave_pairs(st_ck).reshape(lead + (GQA_KV_HEADS, HEAD_DIM)),
        st_cv.reshape(lead + (GQA_KV_HEADS, HEAD_DIM)),
    )
```

```python
import functools
import math

import jax
import jax.numpy as jnp
import numpy as np
from jax import lax
from jax.experimental import pallas as pl
from jax.experimental.pallas import tpu as pltpu

D_MODEL = 1024
BATCH = 32
SEQ = 256
DEPTH = 4
DEC_BATCH = 4
DEC_SEQ = 2048
PAST_LEN = 256
GRID_W = 64
HEAD_DIM = 64
DIFF_HEADS = 4
DIFF_V_DIM = 2 * HEAD_DIM
MLA_HEADS = 8
MLA_Q_RANK = 256
MLA_KV_RANK = 128
MLA_NOPE_DIM = 64
MLA_ROPE_DIM = 32
MLA_QK_DIM = MLA_NOPE_DIM + MLA_ROPE_DIM
MLA_V_DIM = 64
GQA_HEADS = 8
GQA_KV_HEADS = 2
D_FF = 2816
ROPE_BASE = 10000.0
NORM_EPS = 1e-6
LOG2E = math.log2(math.e)

LANES = 128
MXU_N = 256
VMEM_LIMIT_BYTES = 56 * 1024 * 1024

F32 = jnp.float32
BF16 = jnp.bfloat16

C_AQ, C_AK, C_AV = 0, 512, 1024
C_BQD, C_CKV, C_KPE = 1536, 1792, 1920
C_CQ, C_CK, C_CV = 2048, 2560, 2816
W1_COLS = 3072
IN_GATES_COL = 2720

G_BQ, G_BK, G_AQ, G_AK, G_AV, G_BV, G_CQ, G_CK, G_CV = 0, 8, 16, 20, 24, 28, 32, 36, 38
N_GROUPS = 40
STATE_WIDTHS = (512, 512, LANES, LANES, LANES, LANES)
N_STATES = len(STATE_WIDTHS)

V_NORM_MIX, V_NORM_FFN, V_AQ, V_AK, V_BQD, V_BQ, V_CKV, V_BK, V_CQ, V_CK, V_AOUT = range(11)
N_VEC_ROWS = 16

M_SHIFT_M, M_SCALE_M, M_GATE_M, M_SHIFT_F, M_SCALE_F, M_GATE_F = range(6)
N_MOD_ROWS = 8

TM_IN = 256
TM_IN_LATENT = 512
TQ = 512
CTX_SEQS_PER_STEP = 4
TM_MERGE = 512
TM_FFN = 1024
FFN_CONV_ROWS = 128
FFN_HALO = 16
TF = 256
N_FF_CHUNKS = D_FF // TF
FFN_SLOTS = 3
TN_ADA = 1536


def _cparams(sem, flags=None):
    return pltpu.CompilerParams(dimension_semantics=sem, vmem_limit_bytes=VMEM_LIMIT_BYTES, flags=flags)


def _rms(x, gain):
    ms = jnp.mean(x * x, axis=-1, keepdims=True)
    return x * lax.rsqrt(ms + NORM_EPS) * gain


def _lane_lo(rows):
    return lax.broadcasted_iota(jnp.int32, (rows, LANES), 1) < HEAD_DIM


def _first_head_lanes(rows):
    lane = lax.broadcasted_iota(jnp.int32, (rows, LANES), 1)
    return (lane & (HEAD_DIM // 2)) == 0


def _same_head_matrix():
    r = lax.broadcasted_iota(jnp.int32, (LANES, LANES), 0)
    c = lax.broadcasted_iota(jnp.int32, (LANES, LANES), 1)
    same = (r & (HEAD_DIM // 2)) == (c & (HEAD_DIM // 2))
    return jnp.where(same, 1.0, 0.0).astype(BF16)


def _head_norms_lane_sum(groups, gains, width):
    outs = []
    for g, gain in zip(groups, gains):
        ssq = jnp.sum(g * g, axis=-1, keepdims=True)
        outs.append(g * lax.rsqrt(ssq * (1.0 / width) + NORM_EPS) * gain)
    return outs


def _pair_norms_lane_sum(groups, gains, lo):
    outs = []
    for g, gain in zip(groups, gains):
        sq = g * g
        s_all = jnp.sum(sq, axis=-1, keepdims=True)
        s_lo = jnp.sum(jnp.where(lo, sq, 0.0), axis=-1, keepdims=True)
        ssq = jnp.where(lo, s_lo, s_all - s_lo)
        outs.append(g * lax.rsqrt(ssq * (1.0 / HEAD_DIM) + NORM_EPS) * gain)
    return outs


def _head_norms(groups, gains, ones_mat, width):
    rows = groups[0].shape[0]
    sq = jnp.concatenate([(g * g).astype(BF16) for g in groups], axis=0)
    ssq = jnp.dot(sq, ones_mat, preferred_element_type=F32)
    outs = []
    for i, (g, gain) in enumerate(zip(groups, gains)):
        s = ssq[i * rows:(i + 1) * rows]
        outs.append(g * lax.rsqrt(s * (1.0 / width) + NORM_EPS) * gain)
    return outs


def _rope(xg, c, s):
    return xg * c + pltpu.roll(xg, LANES // 2, 1) * s


def _sigmoid(z):
    return 1.0 / (1.0 + jnp.exp(-z))


def _half_heads(qg, lo):
    qf = qg.astype(F32)
    return jnp.where(lo, qf, 0.0).astype(BF16), jnp.where(lo, 0.0, qf).astype(BF16)


def _mods_kernel(c_ref, w_ref, b_ref, o_ref):
    c = c_ref[...]
    a = c * _sigmoid(c)
    w = w_ref[...].astype(BF16)
    a_hi = a.astype(BF16)
    a_lo = (a - a_hi.astype(F32)).astype(BF16)
    acc = jnp.dot(a_hi, w, preferred_element_type=F32) + jnp.dot(a_lo, w, preferred_element_type=F32)
    o_ref[...] = acc + b_ref[...]


def _mods(cvec, w_ada, b_ada):
    n = 6 * D_MODEL
    return pl.pallas_call(
        _mods_kernel,
        out_shape=jax.ShapeDtypeStruct((DEPTH, N_MOD_ROWS, n), F32),
        grid=(DEPTH, n // TN_ADA),
        in_specs=[
            pl.BlockSpec((N_MOD_ROWS, D_MODEL), lambda l, j: (0, 0)),
            pl.BlockSpec((None, D_MODEL, TN_ADA), lambda l, j: (l, 0, j)),
            pl.BlockSpec((None, 1, TN_ADA), lambda l, j: (l, 0, j)),
        ],
        out_specs=pl.BlockSpec((None, N_MOD_ROWS, TN_ADA), lambda l, j: (l, 0, j)),
        compiler_params=_cparams(("arbitrary", "arbitrary")),
        name="ada_mods",
    )(cvec, w_ada, b_ada.reshape(DEPTH, 1, n))


def _mla_head_norms(groups, gains, on_mxu):
    if on_mxu:
        return _head_norms(groups, gains, jnp.ones((LANES, LANES), BF16), MLA_QK_DIM)
    return _head_norms_lane_sum(groups, gains, MLA_QK_DIM)


def _mla_expand(ckv, kpe_grp, wkv_ref, vec_ref, on_mxu=False):
    ckvn = _rms(ckv, vec_ref[V_CKV:V_CKV + 1, 0:MLA_KV_RANK]).astype(BF16)
    kv = jnp.dot(ckvn, wkv_ref[...], preferred_element_type=F32)
    kpre = [kv[:, h * LANES:(h + 1) * LANES] + kpe_grp for h in range(MLA_HEADS)]
    gains = [vec_ref[V_BK:V_BK + 1, h * LANES:(h + 1) * LANES] for h in range(MLA_HEADS)]
    return _mla_head_norms(kpre, gains, on_mxu), kv[:, MLA_HEADS * LANES:]


def _ctx_mla_kernel(ckv_ref, kpe_ref, wkv_ref, vec_ref, k_ref, v_ref):
    keys, vals = _mla_expand(ckv_ref[...], kpe_ref[...], wkv_ref, vec_ref)
    for h in range(MLA_HEADS):
        k_ref[h] = keys[h].astype(BF16)
    for p in range(MLA_HEADS // 2):
        v_ref[p] = vals[:, p * LANES:(p + 1) * LANES].astype(BF16)


def _ctx_mla(cache_ckv, cache_kpe_grp, wkv, vecs):
    return pl.pallas_call(
        _ctx_mla_kernel,
        out_shape=(
            jax.ShapeDtypeStruct((DEPTH, DEC_BATCH, MLA_HEADS, PAST_LEN, LANES), BF16),
            jax.ShapeDtypeStruct((DEPTH, DEC_BATCH, MLA_HEADS // 2, PAST_LEN, LANES), BF16),
        ),
        grid=(DEPTH, DEC_BATCH),
        in_specs=[
            pl.BlockSpec((None, None, PAST_LEN, MLA_KV_RANK), lambda l, b: (b, l, 0, 0)),
            pl.BlockSpec((None, None, PAST_LEN, LANES), lambda l, b: (b, l, 0, 0)),
            pl.BlockSpec((None, MLA_KV_RANK, 12 * LANES), lambda l, b: (l, 0, 0)),
            pl.BlockSpec((None, N_VEC_ROWS, D_MODEL), lambda l, b: (l, 0, 0)),
        ],
        out_specs=(
            pl.BlockSpec((None, None, MLA_HEADS, PAST_LEN, LANES), lambda l, b: (l, b, 0, 0, 0)),
            pl.BlockSpec((None, None, MLA_HEADS // 2, PAST_LEN, LANES), lambda l, b: (l, b, 0, 0, 0)),
        ),
        compiler_params=_cparams(("arbitrary", "arbitrary")),
        name="ctx_mla_expand",
    )(cache_ckv, cache_kpe_grp, wkv, vecs)


def _inproj_kernel(latent, x_ref, mod_ref, vec_ref, w1_ref, wq_ref, wkv_ref, *rest):
    if latent:
        rope_ref, qkv_ref, p_scr = rest
    else:
        qkv_ref, st_ak, st_av, st_ckv, st_kpe, st_ck, st_cv, p_scr = rest[N_STATES:]
    rows = x_ref.shape[0]
    lo = _first_head_lanes(rows)

    x = x_ref[...]
    h = _rms(x, vec_ref[V_NORM_MIX:V_NORM_MIX + 1, :])
    h = h * (1.0 + mod_ref[M_SCALE_M:M_SCALE_M + 1, :]) + mod_ref[M_SHIFT_M:M_SHIFT_M + 1, :]
    p_scr[...] = jnp.dot(h.astype(BF16), w1_ref[...], preferred_element_type=F32)

    def grp(col, g):
        return p_scr[:, col + g * LANES:col + (g + 1) * LANES]

    def vec(row, g):
        return vec_ref[row:row + 1, g * LANES:(g + 1) * LANES]

    def rope64(y):
        if not latent:
            return y
        return _rope(y, rope_ref[0], rope_ref[1])

    def rope_tail(y):
        if not latent:
            return y
        return _rope(y, rope_ref[2], rope_ref[3])

    sections = ((C_AQ, V_AQ, 4), (C_AK, V_AK, 4), (C_CQ, V_CQ, 4), (C_CK, V_CK, 2))
    raw = [grp(col, g) for col, _, n in sections for g in range(n)]
    gains = [vec(row, g) for _, row, n in sections for g in range(n)]
    if latent:
        normed = _head_norms(raw, gains, _same_head_matrix(), HEAD_DIM)
    else:
        normed = _pair_norms_lane_sum(raw, gains, lo)
    aqn, akn, cqn, ckn = normed[0:4], normed[4:8], normed[8:12], normed[12:14]
    for g in range(4):
        qkv_ref[G_AQ + g] = rope64(aqn[g]).astype(BF16)
        qkv_ref[G_AK + g] = rope64(akn[g]).astype(BF16)
        qkv_ref[G_CQ + g] = rope64(cqn[g]).astype(BF16)
        av = grp(C_AV, g)
        qkv_ref[G_AV + g] = av.astype(BF16)
        if not latent:
            st_ak[:, g * LANES:(g + 1) * LANES] = akn[g]
            st_av[:, g * LANES:(g + 1) * LANES] = av
    cvs = [grp(C_CV, g) for g in range(2)]
    for g in range(2):
        qkv_ref[G_CK + g] = rope64(ckn[g]).astype(BF16)
        qkv_ref[G_CV + g] = cvs[g].astype(BF16)
    if not latent:
        st_ck[...] = jnp.where(lo, ckn[0], ckn[1])
        st_cv[...] = jnp.where(_lane_lo(rows), cvs[0], cvs[1])

    bqn = _rms(p_scr[:, C_BQD:C_BQD + MLA_Q_RANK], vec_ref[V_BQD:V_BQD + 1, 0:MLA_Q_RANK]).astype(BF16)
    bq = jnp.dot(bqn, wq_ref[...], preferred_element_type=F32)
    qn = _mla_head_norms([bq[:, hd * LANES:(hd + 1) * LANES] for hd in range(MLA_HEADS)],
                         [vec(V_BQ, hd) for hd in range(MLA_HEADS)], on_mxu=latent)
    ckv = p_scr[:, C_CKV:C_CKV + MLA_KV_RANK]
    kpe_grp = p_scr[:, C_KPE:C_KPE + LANES]
    keys, vals = _mla_expand(ckv, kpe_grp, wkv_ref, vec_ref, on_mxu=latent)
    for hd in range(MLA_HEADS):
        qkv_ref[G_BQ + hd] = rope_tail(qn[hd]).astype(BF16)
        qkv_ref[G_BK + hd] = rope_tail(keys[hd]).astype(BF16)
    for p in range(MLA_HEADS // 2):
        qkv_ref[G_BV + p] = vals[:, p * LANES:(p + 1) * LANES].astype(BF16)
    if not latent:
        st_ckv[...] = ckv
        st_kpe[...] = kpe_grp


def _inproj(x, mod_l, vec_l, w1, wq, wkv, rope_tab, latent, layer=None, states=None):
    t = x.shape[0]
    tm = TM_IN_LATENT if latent else TM_IN
    n_tiles = t // tm
    tiles_per_seq = DEC_SEQ // tm
    if latent:
        mod_idx = lambda i: (1 + i // tiles_per_seq, 0, 0)
    else:
        mod_idx = lambda i: (0, 0, 0)
    const2 = lambda i: (0, 0)
    in_specs = [
        pl.BlockSpec((tm, D_MODEL), lambda i: (i, 0)),
        pl.BlockSpec((None, 6, D_MODEL), mod_idx),
        pl.BlockSpec((N_VEC_ROWS, D_MODEL), const2),
        pl.BlockSpec((D_MODEL, W1_COLS), const2),
        pl.BlockSpec((MLA_Q_RANK, MLA_HEADS * LANES), const2),
        pl.BlockSpec((MLA_KV_RANK, 12 * LANES), const2),
    ]
    args = [x, mod_l, vec_l, w1, wq, wkv]
    qkv_shape = jax.ShapeDtypeStruct((N_GROUPS, t, LANES), BF16)
    qkv_spec = pl.BlockSpec((N_GROUPS, tm, LANES), lambda i: (0, i, 0))
    if latent:
        in_specs.append(pl.BlockSpec((4, tm, LANES), lambda i: (0, i % tiles_per_seq, 0)))
        args.append(rope_tab)
        out_shape = qkv_shape
        out_specs = qkv_spec
    else:
        assert TM_IN == SEQ
        aliases = {len(args) + j: 1 + j for j in range(N_STATES)}
        in_specs += [pl.BlockSpec(memory_space=pl.ANY)] * N_STATES
        args += list(states)
        out_shape = (qkv_shape,) + tuple(jax.ShapeDtypeStruct(st.shape, F32) for st in states)
        out_specs = (qkv_spec,) + tuple(
            pl.BlockSpec((None, None, SEQ, st.shape[-1]), lambda i: (i, layer, 0, 0)) for st in states)
    return pl.pallas_call(
        functools.partial(_inproj_kernel, latent),
        out_shape=out_shape,
        grid=(n_tiles,),
        in_specs=in_specs,
        out_specs=out_specs,
        scratch_shapes=[pltpu.VMEM((tm, W1_COLS), F32)],
        input_output_aliases={} if latent else aliases,
        compiler_params=_cparams(("arbitrary",)),
        name="inproj_latent" if latent else "inproj_context",
    )(*args)


def _attend(q, k_parts, v_parts):
    nt = (((1,), (1,)), ((), ()))
    ss = [lax.dot_general(q, k, nt, preferred_element_type=F32) for k in k_parts]
    m = functools.reduce(jnp.maximum, [jnp.max(s, axis=-1, keepdims=True) for s in ss])
    ps = [jnp.exp2(s - m) for s in ss]
    l = functools.reduce(jnp.add, [jnp.sum(p, axis=-1, keepdims=True) for p in ps])
    o = functools.reduce(
        jnp.add,
        [jnp.dot(p.astype(BF16), v, preferred_element_type=F32) for p, v in zip(ps, v_parts)])
    return o / l


def _seq_rows(ref, n_seq, sq):
    n = ref.shape[1] // n_seq
    return pl.ds(sq * n, n)


def _mla_attn_kernel(has_ctx, n_seq, q_ref, k_ref, v_ref, *rest):
    if has_ctx:
        kc_ref, vc_ref, o_ref = rest
    else:
        (o_ref,) = rest
    lo_out = _lane_lo(q_ref.shape[1] // n_seq)
    for sq in range(n_seq):
        rq, rk = _seq_rows(q_ref, n_seq, sq), _seq_rows(k_ref, n_seq, sq)
        for u in range(MLA_HEADS // 2):
            outs = []
            for half in range(2):
                hd = 2 * u + half
                ks = ([kc_ref[hd]] if has_ctx else []) + [k_ref[hd, rk, :]]
                vs = ([vc_ref[u]] if has_ctx else []) + [v_ref[u, rk, :]]
                outs.append(_attend(q_ref[hd, rq, :], ks, vs))
            o_ref[u, rq, :] = jnp.where(lo_out, outs[0], outs[1]).astype(BF16)


def _gqa_attn_kernel(has_ctx, n_seq, q_ref, k_ref, v_ref, *rest):
    if has_ctx:
        kc_ref, vc_ref, o_ref = rest
    else:
        (o_ref,) = rest
    tq = q_ref.shape[1] // n_seq
    lo_out = _lane_lo(tq)
    lo_q = _first_head_lanes(tq)
    for sq in range(n_seq):
        rq, rk = _seq_rows(q_ref, n_seq, sq), _seq_rows(k_ref, n_seq, sq)
        for u in range(GQA_HEADS // 2):
            kv = u // 2
            ks = ([kc_ref[kv]] if has_ctx else []) + [k_ref[kv, rk, :]]
            vs = ([vc_ref[kv]] if has_ctx else []) + [v_ref[kv, rk, :]]
            q_a, q_b = _half_heads(q_ref[u, rq, :], lo_q)
            o_ref[u, rq, :] = jnp.where(lo_out, _attend(q_a, ks, vs), _attend(q_b, ks, vs)).astype(BF16)


def _diff_attn_kernel(has_ctx, n_seq, lam_init, q_ref, k_ref, v_ref, lam_ref, vec_ref, *rest):
    if has_ctx:
        kc_ref, vc_ref, o_ref = rest
    else:
        (o_ref,) = rest
    lo_q = _first_head_lanes(q_ref.shape[1] // n_seq)
    lv = lam_ref[...]
    lam = (jnp.exp(jnp.sum(lv[0:1] * lv[1:2], axis=-1, keepdims=True))
           - jnp.exp(jnp.sum(lv[2:3] * lv[3:4], axis=-1, keepdims=True)) + lam_init)
    gain = vec_ref[V_AOUT:V_AOUT + 1, 0:LANES]
    for sq in range(n_seq):
        rq, rk = _seq_rows(q_ref, n_seq, sq), _seq_rows(k_ref, n_seq, sq)
        for hd in range(DIFF_HEADS):
            pp, half = hd // 2, hd % 2
            vs = ([vc_ref[hd]] if has_ctx else []) + [v_ref[hd, rk, :]]
            os = []
            for m in range(2):
                g = 2 * m + pp
                q = _half_heads(q_ref[g, rq, :], lo_q)[half]
                ks = ([kc_ref[g]] if has_ctx else []) + [k_ref[g, rk, :]]
                os.append(_attend(q, ks, vs))
            o_ref[hd, rq, :] = _rms(os[0] - lam * os[1], gain).astype(BF16)


def _context_attn_kernel(n_seq, lam_init, aq, ak, av, lam_ref, vec_ref, bq, bk, bv, cq, ck, cv,
                         oa_ref, ob_ref, oc_ref):
    _diff_attn_kernel(False, n_seq, lam_init, aq, ak, av, lam_ref, vec_ref, oa_ref)
    _mla_attn_kernel(False, n_seq, bq, bk, bv, ob_ref)
    _gqa_attn_kernel(False, n_seq, cq, ck, cv, oc_ref)


def _attn_specs(latent, groups_q, blk_q, groups_k, blk_k, groups_v, blk_v):
    if latent:
        per = DEC_SEQ // TQ
        q_idx = lambda b, j: (blk_q, b * per + j, 0)
        k_idx = lambda b, j: (blk_k, b, 0)
        v_idx = lambda b, j: (blk_v, b, 0)
        rows_q, rows_k = TQ, DEC_SEQ
    else:
        q_idx = lambda i: (blk_q, i, 0)
        k_idx = lambda i: (blk_k, i, 0)
        v_idx = lambda i: (blk_v, i, 0)
        rows_q = rows_k = CTX_SEQS_PER_STEP * SEQ
    return [
        pl.BlockSpec((groups_q, rows_q, LANES), q_idx),
        pl.BlockSpec((groups_k, rows_k, LANES), k_idx),
        pl.BlockSpec((groups_v, rows_k, LANES), v_idx),
    ]


def _ctx_spec(groups, layer):
    return pl.BlockSpec((None, None, groups, PAST_LEN, LANES), lambda b, j: (layer, b, 0, 0, 0))


def _attn_call(kernel, name, latent, specs, args, t):
    if latent:
        grid = (DEC_BATCH, DEC_SEQ // TQ)
        o_idx = lambda b, j: (0, b * (DEC_SEQ // TQ) + j, 0)
        sem = ("arbitrary", "arbitrary")
        rows = TQ
    else:
        rows = CTX_SEQS_PER_STEP * SEQ
        grid = (t // rows,)
        o_idx = lambda i: (0, i, 0)
        sem = ("arbitrary",)
    return pl.pallas_call(
        kernel,
        out_shape=jax.ShapeDtypeStruct((4, t, LANES), BF16),
        grid=grid,
        in_specs=specs,
        out_specs=pl.BlockSpec((4, rows, LANES), o_idx),
        compiler_params=_cparams(sem),
        name=name + ("_latent" if latent else "_context"),
    )(*args)


def _attention(qkv, layer, lam_init, lam_l, vec_l, ctx, latent):
    t = qkv.shape[1]
    n_seq = 1 if latent else CTX_SEQS_PER_STEP
    if latent:
        const2 = lambda b, j: (0, 0)
    else:
        const2 = lambda i: (0, 0)

    if not latent:
        specs = _attn_specs(False, 4, G_AQ // 4, 4, G_AK // 4, 4, G_AV // 4)
        specs += [pl.BlockSpec((4, HEAD_DIM), const2), pl.BlockSpec((N_VEC_ROWS, D_MODEL), const2)]
        specs += _attn_specs(False, 8, G_BQ // 8, 8, G_BK // 8, 4, G_BV // 4)
        specs += _attn_specs(False, 4, G_CQ // 4, 2, G_CK // 2, 2, G_CV // 2)
        rows = CTX_SEQS_PER_STEP * SEQ
        o_shape = jax.ShapeDtypeStruct((4, t, LANES), BF16)
        o_spec = pl.BlockSpec((4, rows, LANES), lambda i: (0, i, 0))
        return pl.pallas_call(
            functools.partial(_context_attn_kernel, n_seq, lam_init),
            out_shape=(o_shape, o_shape, o_shape),
            grid=(t // rows,),
            in_specs=specs,
            out_specs=(o_spec, o_spec, o_spec),
            compiler_params=_cparams(("arbitrary",)),
            name="attn_context",
        )(qkv, qkv, qkv, lam_l, vec_l, qkv, qkv, qkv, qkv, qkv, qkv)

    specs = _attn_specs(latent, 4, G_AQ // 4, 4, G_AK // 4, 4, G_AV // 4)
    specs += [pl.BlockSpec((4, HEAD_DIM), const2), pl.BlockSpec((N_VEC_ROWS, D_MODEL), const2)]
    args = [qkv, qkv, qkv, lam_l, vec_l]
    if latent:
        specs += [_ctx_spec(4, layer), _ctx_spec(4, layer)]
        args += [ctx["ak"], ctx["av"]]
    oa = _attn_call(functools.partial(_diff_attn_kernel, latent, n_seq, lam_init), "diff_attn", latent,
                    specs, args, t)

    specs = _attn_specs(latent, 8, G_BQ // 8, 8, G_BK // 8, 4, G_BV // 4)
    args = [qkv, qkv, qkv]
    if latent:
        specs += [_ctx_spec(8, layer), _ctx_spec(4, layer)]
        args += [ctx["bk"], ctx["bv"]]
    ob = _attn_call(functools.partial(_mla_attn_kernel, latent, n_seq), "mla_attn", latent, specs, args, t)

    specs = _attn_specs(latent, 4, G_CQ // 4, 2, G_CK // 2, 2, G_CV // 2)
    args = [qkv, qkv, qkv]
    if latent:
        specs += [_ctx_spec(2, layer), _ctx_spec(2, layer)]
        args += [ctx["ck"], ctx["cv"]]
    oc = _attn_call(functools.partial(_gqa_attn_kernel, latent, n_seq), "gqa_attn", latent, specs, args, t)
    return oa, ob, oc


def _merge_kernel(x_ref, mod_ref, vec_ref, oa_ref, ob_ref, oc_ref, wg_ref, wa_ref, wb_ref, wc_ref,
                  wo_ref, out_ref):
    x = x_ref[...]
    h = _rms(x, vec_ref[V_NORM_MIX:V_NORM_MIX + 1, :])
    h = h * (1.0 + mod_ref[M_SCALE_M:M_SCALE_M + 1, :]) + mod_ref[M_SHIFT_M:M_SHIFT_M + 1, :]
    hb = h.astype(BF16)
    merged = None
    for br, (o_ref, w_ref) in enumerate(((oa_ref, wa_ref), (ob_ref, wb_ref), (oc_ref, wc_ref))):
        o = jnp.concatenate([o_ref[g] for g in range(4)], axis=1)
        proj = jnp.dot(o, w_ref[...], preferred_element_type=F32)
        gate = _sigmoid(jnp.dot(hb, wg_ref[:, br * D_MODEL:(br + 1) * D_MODEL],
                                preferred_element_type=F32))
        merged = gate * proj if merged is None else merged + gate * proj
    mix = jnp.dot(merged.astype(BF16), wo_ref[...], preferred_element_type=F32)
    out_ref[...] = x + mod_ref[M_GATE_M:M_GATE_M + 1, :] * mix


def _merge(x, mod_l, vec_l, oa, ob, oc, wg, wa, wb, wc, wo, latent):
    t = x.shape[0]
    per = DEC_SEQ // TM_MERGE
    mod_idx = (lambda i: (1 + i // per, 0, 0)) if latent else (lambda i: (0, 0, 0))
    const2 = lambda i: (0, 0)
    o_spec = pl.BlockSpec((4, TM_MERGE, LANES), lambda i: (0, i, 0))
    width = 4 * LANES
    return pl.pallas_call(
        _merge_kernel,
        out_shape=jax.ShapeDtypeStruct((t, D_MODEL), F32),
        grid=(t // TM_MERGE,),
        in_specs=[
            pl.BlockSpec((TM_MERGE, D_MODEL), lambda i: (i, 0)),
            pl.BlockSpec((None, 6, D_MODEL), mod_idx),
            pl.BlockSpec((N_VEC_ROWS, D_MODEL), const2),
            o_spec, o_spec, o_spec,
            pl.BlockSpec((D_MODEL, 3 * D_MODEL), const2),
            pl.BlockSpec((width, D_MODEL), const2),
            pl.BlockSpec((width, D_MODEL), const2),
            pl.BlockSpec((width, D_MODEL), const2),
            pl.BlockSpec((D_MODEL, D_MODEL), const2),
        ],
        out_specs=pl.BlockSpec((TM_MERGE, D_MODEL), lambda i: (i, 0)),
        compiler_params=_cparams(("arbitrary",)),
        name="merge_latent" if latent else "merge_context",
    )(x, mod_l, vec_l, oa, ob, oc, wg, wa, wb, wc, wo)


def _zero_rows(a, zero_rows):
    if not zero_rows:
        return a
    sub = lax.broadcasted_iota(jnp.int32, (8, a.shape[1]), 0)
    pieces, cur = [], 0
    for r in sorted(zero_rows):
        base = (r // 8) * 8
        if base > cur:
            pieces.append(a[cur:base])
        pieces.append(jnp.where(sub == (r % 8), 0.0, a[base:base + 8]))
        cur = base + 8
    if cur < a.shape[0]:
        pieces.append(a[cur:])
    return jnp.concatenate(pieces, axis=0)


def _ffn_kernel(seq_len, x_ref, xp_ref, xn_ref, mod_ref, vec_ref, wu_ref, cw_ref, cb_ref, wd_ref,
                out_ref, h_scr, u_scr, act_scr):
    rows = x_ref.shape[0]
    ext = rows + 2 * FFN_HALO
    gain = vec_ref[V_NORM_FFN:V_NORM_FFN + 1, :]
    scale = 1.0 + mod_ref[M_SCALE_F:M_SCALE_F + 1, :]
    shift = mod_ref[M_SHIFT_F:M_SHIFT_F + 1, :]

    def hmod(x):
        return _rms(x, gain) * scale + shift

    row0 = pl.program_id(0) * rows
    keep_prev = (row0 & (seq_len - 1)) != 0
    keep_next = ((row0 + rows) & (seq_len - 1)) != 0
    h_scr[0:FFN_HALO, :] = jnp.where(keep_prev, hmod(xp_ref[...]), 0.0).astype(BF16)
    h_scr[FFN_HALO:FFN_HALO + rows, :] = hmod(x_ref[...]).astype(BF16)
    h_scr[FFN_HALO + rows:ext, :] = jnp.where(keep_next, hmod(xn_ref[...]), 0.0).astype(BF16)
    starts = list(range(seq_len, rows, seq_len))
    ends = [r - 1 for r in starts]

    def up(slot, k):
        for b in range(2):
            col = pl.multiple_of((b * N_FF_CHUNKS + k) * TF, TF)
            u_scr[slot, b] = jnp.dot(h_scr[...], wu_ref[:, pl.ds(col, TF)], preferred_element_type=F32)

    def conv(slot, k):
        pad = 8
        for r0 in range(0, rows, FFN_CONV_ROWS):
            blk_starts = [r - r0 for r in starts if r0 <= r < r0 + FFN_CONV_ROWS]
            blk_ends = [r - r0 for r in ends if r0 <= r < r0 + FFN_CONV_ROWS]

            def branch(b, j):
                lo_row = FFN_HALO + r0 - pad
                u = u_scr[slot, b, lo_row:lo_row + FFN_CONV_ROWS + 2 * pad, :]
                n = FFN_CONV_ROWS + 2 * pad
                prev = _zero_rows(pltpu.roll(u, 1, 0)[pad:pad + FFN_CONV_ROWS], blk_starts)
                nxt = _zero_rows(pltpu.roll(u, n - 1, 0)[pad:pad + FFN_CONV_ROWS], blk_ends)
                cur = u[pad:pad + FFN_CONV_ROWS]
                cw = cw_ref[j]
                return prev * cw[0:1, :] + cur * cw[1:2, :] + nxt * cw[2:3, :] + cb_ref[j]

            a = branch(0, k)
            g = branch(1, N_FF_CHUNKS + k)
            act_scr[slot, r0:r0 + FFN_CONV_ROWS, :] = (a * (g * _sigmoid(g))).astype(BF16)

    def down(slot, k):
        for n0 in range(0, D_MODEL, MXU_N):
            out_ref[:, n0:n0 + MXU_N] += jnp.dot(act_scr[slot], wd_ref[k, :, n0:n0 + MXU_N],
                                                 preferred_element_type=F32)

    def stage(k, k_mod):
        up(k_mod % FFN_SLOTS, k)
        conv((k_mod - 1) % FFN_SLOTS, k - 1)
        down((k_mod - 2) % FFN_SLOTS, k - 2)

    out_ref[...] = jnp.zeros_like(out_ref)
    up(0, 0)
    up(1, 1)
    conv(0, 0)

    def body(j, carry):
        k = 2 + FFN_SLOTS * j
        for i in range(FFN_SLOTS):
            stage(k + i, 2 + i)
        return carry

    last = N_FF_CHUNKS - 1
    n_trips = (last - 1) // FFN_SLOTS
    assert 2 + n_trips * FFN_SLOTS == last + 1
    lax.fori_loop(0, n_trips, body, 0)
    conv(last % FFN_SLOTS, last)
    down((last - 1) % FFN_SLOTS, last - 1)
    down(last % FFN_SLOTS, last)
    out_ref[...] = x_ref[...] + mod_ref[M_GATE_F:M_GATE_F + 1, :] * out_ref[...]


def _ffn(x, mod_l, vec_l, wu, cw, cb, wd, latent):
    t = x.shape[0]
    per = DEC_SEQ // TM_FFN
    mod_idx = (lambda i: (1 + i // per, 0, 0)) if latent else (lambda i: (0, 0, 0))
    seq_len = DEC_SEQ if latent else SEQ
    halo_per_tile = TM_FFN // FFN_HALO
    n_halo = t // FFN_HALO
    const3 = lambda i: (0, 0, 0)
    resident = pl.Buffered(1)
    return pl.pallas_call(
        functools.partial(_ffn_kernel, seq_len),
        out_shape=jax.ShapeDtypeStruct((t, D_MODEL), F32),
        grid=(t // TM_FFN,),
        in_specs=[
            pl.BlockSpec((TM_FFN, D_MODEL), lambda i: (i, 0)),
            pl.BlockSpec((FFN_HALO, D_MODEL), lambda i: (jnp.maximum(i * halo_per_tile - 1, 0), 0)),
            pl.BlockSpec((FFN_HALO, D_MODEL),
                         lambda i: (jnp.minimum((i + 1) * halo_per_tile, n_halo - 1), 0)),
            pl.BlockSpec((None, 6, D_MODEL), mod_idx),
            pl.BlockSpec((N_VEC_ROWS, D_MODEL), lambda i: (0, 0)),
            pl.BlockSpec((D_MODEL, 2 * D_FF), lambda i: (0, 0), pipeline_mode=resident),
            pl.BlockSpec((2 * N_FF_CHUNKS, 3, TF), const3),
            pl.BlockSpec((2 * N_FF_CHUNKS, 1, TF), const3),
            pl.BlockSpec((N_FF_CHUNKS, TF, D_MODEL), const3, pipeline_mode=resident),
        ],
        out_specs=pl.BlockSpec((TM_FFN, D_MODEL), lambda i: (i, 0)),
        scratch_shapes=[pltpu.VMEM((TM_FFN + 2 * FFN_HALO, D_MODEL), BF16),
                        pltpu.VMEM((FFN_SLOTS, 2, TM_FFN + 2 * FFN_HALO, TF), F32),
                        pltpu.VMEM((FFN_SLOTS, TM_FFN, TF), BF16)],
        compiler_params=_cparams(("arbitrary",)),
        name="ffn_latent" if latent else "ffn_context",
    )(x, x, x, mod_l, vec_l, wu, cw, cb, wd)


def _rope_angles(n_tokens, dim):
    rows = n_tokens // GRID_W
    row = jnp.repeat(jnp.arange(rows, dtype=F32), GRID_W)
    col = jnp.tile(jnp.arange(GRID_W, dtype=F32), rows)
    quarter = dim // 4
    inv = 1.0 / (ROPE_BASE ** (jnp.arange(quarter, dtype=F32) / quarter))
    ang = jnp.concatenate([row[:, None] * inv, col[:, None] * inv], axis=-1)
    return jnp.cos(ang), jnp.sin(ang)


def _rope_tables():
    t = DEC_SEQ
    cos_h, sin_h = _rope_angles(t, HEAD_DIM)
    c64 = jnp.tile(cos_h, (1, 4))
    s64 = jnp.concatenate([-sin_h, -sin_h, sin_h, sin_h], axis=1)
    cos_r, sin_r = _rope_angles(t, MLA_ROPE_DIM)
    fill = LANES // 2 - MLA_ROPE_DIM // 2
    ones, zeros = jnp.ones((t, fill), F32), jnp.zeros((t, fill), F32)
    cm = jnp.concatenate([cos_r, ones, cos_r, ones], axis=1)
    sm = jnp.concatenate([-sin_r, zeros, sin_r, zeros], axis=1)
    return jnp.stack([c64, s64, cm, sm])


def _interleave_pairs(w):
    lead = w.shape[:-1]
    w = w.reshape(lead + (-1, 2, 2, HEAD_DIM // 2))
    return jnp.swapaxes(w, -2, -3).reshape(lead + (-1,))


def _place96(x):
    half = MLA_ROPE_DIM // 2
    cut = LANES // 2 - half
    pad = jnp.zeros(x.shape[:-1] + (LANES - MLA_QK_DIM,), x.dtype)
    return jnp.concatenate(
        [x[..., MLA_NOPE_DIM:MLA_NOPE_DIM + half], x[..., :cut], x[..., MLA_NOPE_DIM + half:],
         x[..., cut:MLA_NOPE_DIM], pad], axis=-1)


def _place_nope(x):
    rope = jnp.zeros(x.shape[:-1] + (MLA_ROPE_DIM,), x.dtype)
    return _place96(jnp.concatenate([x, rope], axis=-1))


def _place_rope(x):
    nope = jnp.zeros(x.shape[:-1] + (MLA_NOPE_DIM,), x.dtype)
    return _place96(jnp.concatenate([nope, x], axis=-1))


def _unplace_rope(x):
    half = MLA_ROPE_DIM // 2
    return jnp.concatenate([x[..., :half], x[..., LANES // 2:LANES // 2 + half]], axis=-1)


def _heads96(v):
    return jnp.tile(_place96(v), MLA_HEADS)


def _vec_pack(l, lam_init, norm_mix, norm_ffn, diff_qk_norm, mla_q_norm, mla_kv_norm,
              mla_qk_norm, gqa_qk_norm, diff_out_norm):
    pair = _interleave_pairs

    def row(v):
        return jnp.pad(v.astype(F32), (0, D_MODEL - v.shape[0]))
    rows = [None] * N_VEC_ROWS
    rows[V_NORM_MIX] = row(norm_mix[l])
    rows[V_NORM_FFN] = row(norm_ffn[l])
    rows[V_AQ] = row(pair(jnp.tile(diff_qk_norm[l, 0], 8)) * HEAD_DIM ** -0.5 * LOG2E)
    rows[V_AK] = row(pair(jnp.tile(diff_qk_norm[l, 1], 8)))
    rows[V_BQD] = row(mla_q_norm[l])
    rows[V_BQ] = row(_heads96(mla_qk_norm[l, 0]) * MLA_QK_DIM ** -0.5 * LOG2E)
    rows[V_CKV] = row(mla_kv_norm[l])
    rows[V_BK] = row(_heads96(mla_qk_norm[l, 1]))
    rows[V_CQ] = row(pair(jnp.tile(gqa_qk_norm[l, 0], 8)) * HEAD_DIM ** -0.5 * LOG2E)
    rows[V_CK] = row(pair(jnp.tile(gqa_qk_norm[l, 1], 4)))
    rows[V_AOUT] = row(diff_out_norm[l] * (1.0 - lam_init))
    zero = jnp.zeros((D_MODEL,), F32)
    return jnp.stack([r if r is not None else zero for r in rows])


def _dup_heads64(w, n_heads):
    lead = w.shape[:-1]
    w = w.reshape(lead + (n_heads, 1, HEAD_DIM))
    return jnp.broadcast_to(w, lead + (n_heads, 2, HEAD_DIM)).reshape(lead + (n_heads * LANES,))


def _w1_layout(w_in_l):
    pair = _interleave_pairs
    aq, ak, av = w_in_l[:, 0:512], w_in_l[:, 512:1024], w_in_l[:, 1024:1536]
    bqd = w_in_l[:, 1536:1792]
    ckv = w_in_l[:, 1792:1920]
    kpe = w_in_l[:, 1920:1952]
    cq = w_in_l[:, 1952:2464]
    ck = w_in_l[:, 2464:2592]
    cv = w_in_l[:, 2592:2720]
    w1 = jnp.concatenate(
        [pair(aq), pair(ak), av, bqd, ckv, _place_rope(kpe), pair(cq),
         pair(_dup_heads64(ck, GQA_KV_HEADS)), _dup_heads64(cv, GQA_KV_HEADS)], axis=1)
    return w1.astype(BF16)


def _wq_layout(w_q_up_l):
    w = _place96(w_q_up_l.reshape(MLA_Q_RANK, MLA_HEADS, MLA_QK_DIM))
    return w.reshape(MLA_Q_RANK, MLA_HEADS * LANES).astype(BF16)


def _wkv_layout(w_kv_up_l):
    w = w_kv_up_l.reshape(MLA_KV_RANK, MLA_HEADS, MLA_NOPE_DIM + MLA_V_DIM)
    k = _place_nope(w[:, :, :MLA_NOPE_DIM])
    v = w[:, :, MLA_NOPE_DIM:]
    return jnp.concatenate(
        [k.reshape(MLA_KV_RANK, MLA_HEADS * LANES), v.reshape(MLA_KV_RANK, MLA_HEADS * MLA_V_DIM)],
        axis=1).astype(BF16)


def _groups(x, n_groups):
    b, l, s, _ = x.shape
    return jnp.transpose(x.reshape(b, l, s, n_groups, LANES), (1, 0, 3, 2, 4)).astype(BF16)


def kernel(x_prompt, x_sample, c, cache_diff_k, cache_diff_v, cache_mla_ckv, cache_mla_kpe,
           cache_gqa_k, cache_gqa_v, c_ctx, w_ada, b_ada, norm_mix, w_in, diff_qk_norm,
           diff_lambda, diff_out_norm, mla_q_norm, w_mla_q_up, mla_kv_norm, w_mla_kv_up,
           mla_qk_norm, gqa_qk_norm, w_branch_a, w_branch_b, w_branch_c, w_o, norm_ffn,
           w_up, conv_w, conv_b, w_down):
    lam_inits = [0.8 - 0.6 * math.exp(-0.3 * l) for l in range(DEPTH)]
    vecs = jnp.stack([
        _vec_pack(l, lam_inits[l], norm_mix, norm_ffn, diff_qk_norm, mla_q_norm, mla_kv_norm,
                  mla_qk_norm, gqa_qk_norm, diff_out_norm) for l in range(DEPTH)])
    wkv_all = jnp.stack([_wkv_layout(w_mla_kv_up[l]) for l in range(DEPTH)])

    cvec = jnp.concatenate(
        [c_ctx[None, :], c, jnp.zeros((N_MOD_ROWS - 1 - DEC_BATCH, D_MODEL), F32)], axis=0)
    mods = _mods(cvec, w_ada, b_ada).reshape(DEPTH, N_MOD_ROWS, 6, D_MODEL)

    rope_tab = _rope_tables()

    ctx_bk, ctx_bv = _ctx_mla(cache_mla_ckv, _place_rope(cache_mla_kpe), wkv_all, vecs)
    nb, nl, ns = DEC_BATCH, DEPTH, PAST_LEN
    ctx = {
        "ak": _groups(_interleave_pairs(cache_diff_k.reshape(nb, nl, ns, 512)), 4),
        "av": _groups(cache_diff_v.reshape(nb, nl, ns, 512), 4),
        "bk": ctx_bk,
        "bv": ctx_bv,
        "ck": _groups(_interleave_pairs(_dup_heads64(cache_gqa_k.reshape(nb, nl, ns, 128), GQA_KV_HEADS)), 2),
        "cv": _groups(_dup_heads64(cache_gqa_v.reshape(nb, nl, ns, 128), GQA_KV_HEADS), 2),
    }

    y_p = x_prompt.reshape(BATCH * SEQ, D_MODEL)
    y_s = x_sample.reshape(DEC_BATCH * DEC_SEQ, D_MODEL)
    states = tuple(jnp.zeros((BATCH, DEPTH, SEQ, w), F32) for w in STATE_WIDTHS)
    for l in range(DEPTH):
        w1 = _w1_layout(w_in[l])
        wg = w_in[l][:, IN_GATES_COL:].astype(BF16)
        wq = _wq_layout(w_mla_q_up[l])
        wkv = wkv_all[l]
        wa, wb, wc = (w.astype(BF16) for w in (w_branch_a[l], w_branch_b[l], w_branch_c[l]))
        wo = w_o[l].astype(BF16)
        nch = 2 * N_FF_CHUNKS
        wu = w_up[l].astype(BF16)
        wd = w_down[l].astype(BF16).reshape(N_FF_CHUNKS, TF, D_MODEL)
        cw = jnp.transpose(conv_w[l].reshape(3, nch, TF), (1, 0, 2))
        cb = conv_b[l].reshape(nch, 1, TF)
        vec_l, mod_l, lam_l = vecs[l], mods[l], diff_lambda[l]

        qkv, *states = _inproj(y_p, mod_l, vec_l, w1, wq, wkv, None, latent=False, layer=l, states=states)
        oa, ob, oc = _attention(qkv, l, lam_inits[l], lam_l, vec_l, None, latent=False)
        y_p = _merge(y_p, mod_l, vec_l, oa, ob, oc, wg, wa, wb, wc, wo, latent=False)
        y_p = _ffn(y_p, mod_l, vec_l, wu, cw, cb, wd, latent=False)

        qkv = _inproj(y_s, mod_l, vec_l, w1, wq, wkv, rope_tab, latent=True)
        oa, ob, oc = _attention(qkv, l, lam_inits[l], lam_l, vec_l, ctx, latent=True)
        y_s = _merge(y_s, mod_l, vec_l, oa, ob, oc, wg, wa, wb, wc, wo, latent=True)
        y_s = _ffn(y_s, mod_l, vec_l, wu, cw, cb, wd, latent=True)

    st_ak, st_av, st_ckv, st_kpe, st_ck, st_cv = states
    lead = (BATCH, DEPTH, SEQ)
    return (
        y_p.reshape(BATCH, SEQ, D_MODEL),
        y_s.reshape(DEC_BATCH, DEC_SEQ, D_MODEL),
        _interleave_pairs(st_ak).reshape(lead + (2, DIFF_HEADS, HEAD_DIM)),
        st_av.reshape(lead + (DIFF_HEADS, DIFF_V_DIM)),
        st_ckv,
        _unplace_rope(st_kpe),
        _interleave_pairs(st_ck).reshape(lead + (GQA_KV_HEADS, HEAD_DIM)),
        st_cv.reshape(lead + (GQA_KV_HEADS, HEAD_DIM)),
    )
```
